```python
import jax, jax.numpy as jnp
from jax import lax
import numpy as np

D_MODEL = 2048
BATCH = 2
SEQ = 4096
DEPTH = 1
DEC_BATCH = 32
DEC_SEQ = 8
PAST_LEN = 16384
PAGE_SIZE = 128

MIX_WIDTH = D_MODEL
ATTN_WIDTH = MIX_WIDTH // 2
GMLP_WIDTH = MIX_WIDTH - ATTN_WIDTH
HEAD_DIM = 64
N_ATTN_HEADS = ATTN_WIDTH // HEAD_DIM
CHUNK = 128
GMLP_HEAD_DIM = 128
N_GMLP_HEADS = GMLP_WIDTH // GMLP_HEAD_DIM
D_FF = ((8 * D_MODEL // 3 + 127) // 128) * 128
Q_BLOCK = 128
EPS = 1e-6
NEG_INF = -1e30
PROJ_SPLITS = (ATTN_WIDTH, 2 * ATTN_WIDTH, 3 * ATTN_WIDTH,
               3 * ATTN_WIDTH + N_ATTN_HEADS,
               3 * ATTN_WIDTH + N_ATTN_HEADS + GMLP_WIDTH)
PROJ_WIDTH = 3 * ATTN_WIDTH + N_ATTN_HEADS + 2 * GMLP_WIDTH
CACHE_FORGET_LOGIT = 8.0

kernel_name = "hymba_fox_gmlp_macaron_step"


def rmsnorm(x, g):
    xf = x.astype(jnp.float32)
    y = xf * lax.rsqrt(jnp.mean(xf * xf, axis=-1, keepdims=True) + EPS)
    return (y * g.astype(jnp.float32)).astype(x.dtype)


def ffn_half(x, g, w_gate, w_up, w_down):
    h = rmsnorm(x, g)
    return x + 0.5 * ((jax.nn.silu(h @ w_gate) * (h @ w_up)) @ w_down)


def project(h, w_in, b_f, g_gv):
    lead = h.shape[:-1]
    q, k, v, zf, u, gv = jnp.split(h @ w_in, PROJ_SPLITS, axis=-1)
    q = q.reshape(*lead, N_ATTN_HEADS, HEAD_DIM)
    k = k.reshape(*lead, N_ATTN_HEADS, HEAD_DIM)
    v = v.reshape(*lead, N_ATTN_HEADS, HEAD_DIM)
    logf = jax.nn.log_sigmoid(zf.astype(jnp.float32) + b_f.astype(jnp.float32))
    u = jax.nn.gelu(u).reshape(*lead, N_GMLP_HEADS, GMLP_HEAD_DIM)
    gvn = rmsnorm(jax.nn.gelu(gv).reshape(*lead, N_GMLP_HEADS, GMLP_HEAD_DIM), g_gv)
    return q, k, v, logf, u, gvn


def fox_prompt(q, k, v, logf):
    B, S, H, Dh = q.shape
    scale = Dh ** -0.5
    c = jnp.cumsum(logf, axis=1)
    cT = c.transpose(0, 2, 1)
    nb = S // Q_BLOCK
    qb = q.reshape(B, nb, Q_BLOCK, H, Dh).swapaxes(0, 1)
    cb = cT.reshape(B, H, nb, Q_BLOCK).transpose(2, 0, 1, 3)
    key_pos = jnp.arange(S)

    def block(args):
        i, q_i, c_i = args
        logits = jnp.einsum('bqhd,bkhd->bhqk', q_i, k,
                            preferred_element_type=jnp.float32) * scale
        bias = c_i[..., :, None] - cT[:, :, None, :]
        q_pos = i * Q_BLOCK + jnp.arange(Q_BLOCK)
        mask = key_pos[None, :] <= q_pos[:, None]
        p = jax.nn.softmax(jnp.where(mask, logits + bias, NEG_INF), axis=-1)
        return jnp.einsum('bhqk,bkhd->bqhd', p.astype(v.dtype), v)

    out = lax.map(block, (jnp.arange(nb), qb, cb))
    return out.swapaxes(0, 1).reshape(B, S, H * Dh)


def fox_sample(q, k, v, logf, cache_k, cache_v, cache_logf, page_table):
    DB, T, H, Dh = q.shape
    scale = Dh ** -0.5
    qf = q.astype(jnp.float32) * scale
    cnew = jnp.cumsum(logf, axis=1).transpose(0, 2, 1)
    self_logits = jnp.einsum('bthd,bshd->bhts', qf, k.astype(jnp.float32))
    self_logits = self_logits + cnew[..., :, None] - cnew[..., None, :]
    causal = jnp.tril(jnp.ones((T, T), dtype=bool))
    self_logits = jnp.where(causal, self_logits, NEG_INF)
    m = jnp.max(self_logits, axis=-1)
    p = jnp.exp(self_logits - m[..., None])
    l = jnp.sum(p, axis=-1)
    acc = jnp.einsum('bhts,bshd->bhtd', p, v.astype(jnp.float32))
    n_pages = page_table.shape[1]
    logf_past = cache_logf[page_table].astype(jnp.float32).reshape(DB, n_pages * PAGE_SIZE, H)
    suffix = lax.cumsum(logf_past, axis=1, reverse=True) - logf_past
    suffix = suffix.reshape(DB, n_pages, PAGE_SIZE, H).transpose(1, 0, 3, 2)

    def step(carry, xs):
        m, l, acc = carry
        pids, suf = xs
        kp = cache_k[pids].astype(jnp.float32)
        vp = cache_v[pids].astype(jnp.float32)
        logits = jnp.einsum('bthd,bshd->bhts', qf, kp) + cnew[..., None] + suf[:, :, None, :]
        m_new = jnp.maximum(m, jnp.max(logits, axis=-1))
        alpha = jnp.exp(m - m_new)
        p = jnp.exp(logits - m_new[..., None])
        l = l * alpha + jnp.sum(p, axis=-1)
        acc = acc * alpha[..., None] + jnp.einsum('bhts,bshd->bhtd', p, vp)
        return (m_new, l, acc), None

    (m, l, acc), _ = lax.scan(step, (m, l, acc), (page_table.T, suffix))
    out = acc / l[..., None]
    return out.transpose(0, 2, 1, 3).reshape(DB, T, H * Dh).astype(q.dtype)


def gmlp_prompt(u, gvn, w_s, b_s):
    B, S, G, C = gvn.shape
    nc = S // CHUNK
    w = w_s * jnp.tril(jnp.ones((CHUNK, CHUNK), dtype=w_s.dtype))
    vc = gvn.reshape(B, nc, CHUNK, G, C)
    mixed = jnp.einsum('gts,bnsgc->bntgc', w, vc) + b_s.T[None, None, :, :, None]
    return (u * mixed.reshape(B, S, G, C)).reshape(B, S, G * C)


def gmlp_sample(u, gvn, w_s, b_s):
    DB, T, G, C = gvn.shape
    w = w_s[:, :T, :T] * jnp.tril(jnp.ones((T, T), dtype=w_s.dtype))
    mixed = jnp.einsum('gts,bsgc->btgc', w, gvn) + b_s[:, :T].T[None, :, :, None]
    return (u * mixed).reshape(DB, T, G * C)


def setup_inputs(seed: int = 0) -> dict:
    key = jax.random.key(seed)
    ks = jax.random.split(key, 24)
    f32 = jnp.float32
    n_pages = PAST_LEN // PAGE_SIZE
    used = DEC_BATCH * n_pages
    n_pool = used + max(1, used // 4)
    nrm = lambda k, shape, s: jax.random.normal(k, shape, f32) * s
    gain = lambda k, shape: 1.0 + 0.05 * jax.random.normal(k, shape, f32)
    page_table = jax.random.permutation(ks[5], n_pool)[:used].reshape(DEC_BATCH, n_pages).astype(jnp.int32)
    return {
        "x_prompt": nrm(ks[0], (BATCH, SEQ, D_MODEL), 1.0),
        "x_sample": nrm(ks[1], (DEC_BATCH, DEC_SEQ, D_MODEL), 1.0),
        "cache_k": nrm(ks[2], (DEPTH, n_pool, PAGE_SIZE, N_ATTN_HEADS, HEAD_DIM), 1.0),
        "cache_v": nrm(ks[3], (DEPTH, n_pool, PAGE_SIZE, N_ATTN_HEADS, HEAD_DIM), 1.0),
        "cache_logf": jax.nn.log_sigmoid(CACHE_FORGET_LOGIT + 0.5 * jax.random.normal(ks[4], (DEPTH, n_pool, PAGE_SIZE, N_ATTN_HEADS), f32)),
        "page_table": page_table,
        "g_ffn1": gain(ks[6], (DEPTH, D_MODEL)),
        "w1_gate": nrm(ks[7], (DEPTH, D_MODEL, D_FF), D_MODEL ** -0.5),
        "w1_up": nrm(ks[8], (DEPTH, D_MODEL, D_FF), D_MODEL ** -0.5),
        "w1_down": nrm(ks[9], (DEPTH, D_FF, D_MODEL), D_FF ** -0.5),
        "g_mix": gain(ks[10], (DEPTH, D_MODEL)),
        "w_in": nrm(ks[11], (DEPTH, D_MODEL, PROJ_WIDTH), D_MODEL ** -0.5),
        "b_f": jax.random.uniform(ks[12], (DEPTH, N_ATTN_HEADS), f32, 3.0, 6.0),
        "g_gv": gain(ks[13], (DEPTH, N_GMLP_HEADS, GMLP_HEAD_DIM)),
        "w_s": nrm(ks[14], (DEPTH, N_GMLP_HEADS, CHUNK, CHUNK), CHUNK ** -0.5),
        "b_s": gain(ks[15], (DEPTH, N_GMLP_HEADS, CHUNK)),
        "w_out": nrm(ks[16], (DEPTH, MIX_WIDTH, D_MODEL), MIX_WIDTH ** -0.5),
        "g_ffn2": gain(ks[17], (DEPTH, D_MODEL)),
        "w2_gate": nrm(ks[18], (DEPTH, D_MODEL, D_FF), D_MODEL ** -0.5),
        "w2_up": nrm(ks[19], (DEPTH, D_MODEL, D_FF), D_MODEL ** -0.5),
        "w2_down": nrm(ks[20], (DEPTH, D_FF, D_MODEL), D_FF ** -0.5),
        "g_final": gain(ks[21], (D_MODEL,)),
    }


def reference(x_prompt, x_sample, cache_k, cache_v, cache_logf, page_table,
              g_ffn1, w1_gate, w1_up, w1_down, g_mix, w_in, b_f, g_gv, w_s, b_s, w_out,
              g_ffn2, w2_gate, w2_up, w2_down, g_final):
    xp, xs = x_prompt, x_sample
    kp_l, vp_l, lfp_l, ks_l, vs_l, lfs_l, gvs_l = [], [], [], [], [], [], []
    for l in range(DEPTH):
        xp = ffn_half(xp, g_ffn1[l], w1_gate[l], w1_up[l], w1_down[l])
        xs = ffn_half(xs, g_ffn1[l], w1_gate[l], w1_up[l], w1_down[l])
        q, k, v, lf, u, gvn = project(rmsnorm(xp, g_mix[l]), w_in[l], b_f[l], g_gv[l])
        mix = jnp.concatenate([fox_prompt(q, k, v, lf).astype(xp.dtype),
                               gmlp_prompt(u, gvn, w_s[l], b_s[l])], axis=-1)
        xp = xp + mix @ w_out[l]
        kp_l.append(k); vp_l.append(v); lfp_l.append(lf)
        q, k, v, lf, u, gvn = project(rmsnorm(xs, g_mix[l]), w_in[l], b_f[l], g_gv[l])
        att = fox_sample(q, k, v, lf, cache_k[l], cache_v[l], cache_logf[l], page_table)
        mix = jnp.concatenate([att.astype(xs.dtype),
                               gmlp_sample(u, gvn, w_s[l], b_s[l])], axis=-1)
        xs = xs + mix @ w_out[l]
        ks_l.append(k); vs_l.append(v); lfs_l.append(lf); gvs_l.append(gvn)
        xp = ffn_half(xp, g_ffn2[l], w2_gate[l], w2_up[l], w2_down[l])
        xs = ffn_half(xs, g_ffn2[l], w2_gate[l], w2_up[l], w2_down[l])
    y_prompt = rmsnorm(xp, g_final)
    y_sample = rmsnorm(xs, g_final)
    return (y_prompt, y_sample,
            jnp.stack(kp_l), jnp.stack(vp_l), jnp.stack(lfp_l),
            jnp.stack(ks_l), jnp.stack(vs_l), jnp.stack(lfs_l), jnp.stack(gvs_l))
```

```python
import functools

import jax
import jax.numpy as jnp
from jax import lax
from jax.experimental import pallas as pl
from jax.experimental.pallas import tpu as pltpu

f32 = jnp.float32
bf16 = jnp.bfloat16

D_MODEL = 2048
BATCH = 2
SEQ = 4096
DEC_BATCH = 32
DEC_SEQ = 8
PAST_LEN = 16384
PAGE_SIZE = 128
ATTN_WIDTH = 1024
GMLP_WIDTH = 1024
HEAD_DIM = 64
N_HEADS = 16
CHUNK = 128
GMLP_HEAD_DIM = 128
N_GMLP_HEADS = 8
D_FF = 5504
EPS = 1e-6
NEG_INF = -1e30
N_PAGES = PAST_LEN // PAGE_SIZE

LANES = 128
SUBLANES = 8
MXU_DIM = 256

N_PROMPT = BATCH * SEQ
N_SAMPLE = DEC_BATCH * DEC_SEQ
N_ROWS = N_PROMPT + N_SAMPLE
D_FF_PAD = -(-D_FF // (2 * MXU_DIM)) * (2 * MXU_DIM)
PAGE_ROWS = PAGE_SIZE * N_HEADS

FFN_TM = 768
FFN_TF = 512
PROJ_TM = 256
ATT_TQ = 512
PAGES_PER_STEP = 4
SFX_PAGES_PER_STEP = 16
VMEM_LIMIT = 56 * 1024 * 1024


def _rms(x, g):
    return x * lax.rsqrt(jnp.mean(x * x, axis=-1, keepdims=True) + EPS) * g


def _split3(x):
    hi = x.astype(bf16)
    r1 = x - hi.astype(f32)
    mid = r1.astype(bf16)
    lo = (r1 - mid.astype(f32)).astype(bf16)
    return hi, mid, lo


def _dot(a, b):
    return jnp.dot(a, b, preferred_element_type=f32)


def _dot_nt(a, b):
    return lax.dot_general(a, b, (((1,), (1,)), ((), ())), preferred_element_type=f32)


def _ffn_kernel(x_ref, g_ref, wg_ref, wu_ref, wd_ref, gfin_ref, o_ref, h_ref, *, final_norm, nf):
    f = pl.program_id(1)

    @pl.when(f == 0)
    def _():
        x = x_ref[...]
        h_ref[...] = _rms(x, g_ref[...]).astype(bf16)
        o_ref[...] = x

    h = h_ref[...]
    gate = _dot(h, wg_ref[...])
    up = _dot(h, wu_ref[...])
    act = (gate / (1.0 + jnp.exp(-gate))) * up * 0.5
    o_ref[...] += _dot(act.astype(bf16), wd_ref[...])

    if final_norm:
        @pl.when(f == nf - 1)
        def _():
            o_ref[...] = _rms(o_ref[...], gfin_ref[...])


def _ffn(x, g, wg, wu, wd, gfin, final_norm):
    nf = D_FF_PAD // FFN_TF
    kern = functools.partial(_ffn_kernel, final_norm=final_norm, nf=nf)
    return pl.pallas_call(
        kern,
        grid=(N_ROWS // FFN_TM, nf),
        in_specs=[
            pl.BlockSpec((FFN_TM, D_MODEL), lambda i, f: (i, 0)),
            pl.BlockSpec((1, D_MODEL), lambda i, f: (0, 0)),
            pl.BlockSpec((D_MODEL, FFN_TF), lambda i, f: (0, f)),
            pl.BlockSpec((D_MODEL, FFN_TF), lambda i, f: (0, f)),
            pl.BlockSpec((FFN_TF, D_MODEL), lambda i, f: (f, 0)),
            pl.BlockSpec((1, D_MODEL), lambda i, f: (0, 0)),
        ],
        out_specs=pl.BlockSpec((FFN_TM, D_MODEL), lambda i, f: (i, 0)),
        out_shape=jax.ShapeDtypeStruct((N_ROWS, D_MODEL), f32),
        scratch_shapes=[pltpu.VMEM((FFN_TM, D_MODEL), bf16)],
        compiler_params=pltpu.CompilerParams(
            dimension_semantics=("parallel", "arbitrary"), vmem_limit_bytes=VMEM_LIMIT),
        name="ffn_half",
    )(x, g, wg, wu, wd, gfin)


def _aproj_kernel(x_ref, g_ref, wq_ref, wk_ref, wv_ref, wf_ref, bf_ref,
                  q_ref, qb_ref, k_ref, kb_ref, v_ref, vb_ref, lf_ref):
    h = _rms(x_ref[...], g_ref[...]).astype(bf16)
    q = _dot(h, wq_ref[...]) * (HEAD_DIM ** -0.5)
    q_ref[...] = q
    qb_ref[...] = q.astype(bf16)
    k = _dot(h, wk_ref[...])
    k_ref[...] = k
    kb_ref[...] = k.astype(bf16)
    v = _dot(h, wv_ref[...])
    v_ref[...] = v
    vb_ref[...] = v.astype(bf16)
    z = _dot(h, wf_ref[...]) + bf_ref[...]
    lf_ref[...] = -(jnp.maximum(-z, 0.0) + jnp.log1p(jnp.exp(-jnp.abs(z))))


def _attn_proj(x, g, wq, wk, wv, wf, bfp):
    row = lambda i: (i, 0)
    fixed = lambda i: (0, 0)
    wide = lambda dt: jax.ShapeDtypeStruct((N_ROWS, ATTN_WIDTH), dt)
    return pl.pallas_call(
        _aproj_kernel,
        grid=(N_ROWS // PROJ_TM,),
        in_specs=[
            pl.BlockSpec((PROJ_TM, D_MODEL), row),
            pl.BlockSpec((1, D_MODEL), fixed),
            pl.BlockSpec((D_MODEL, ATTN_WIDTH), fixed),
            pl.BlockSpec((D_MODEL, ATTN_WIDTH), fixed),
            pl.BlockSpec((D_MODEL, ATTN_WIDTH), fixed),
            pl.BlockSpec((D_MODEL, LANES), fixed),
            pl.BlockSpec((1, LANES), fixed),
        ],
        out_specs=[pl.BlockSpec((PROJ_TM, ATTN_WIDTH), row)] * 6 + [pl.BlockSpec((PROJ_TM, LANES), row)],
        out_shape=[wide(f32), wide(bf16), wide(f32), wide(bf16), wide(f32), wide(bf16),
                   jax.ShapeDtypeStruct((N_ROWS, LANES), f32)],
        compiler_params=pltpu.CompilerParams(
            dimension_semantics=("parallel",), vmem_limit_bytes=VMEM_LIMIT),
        name="attn_proj",
    )(x, g, wq, wk, wv, wf, bfp)


def _gelu_tanh(x):
    return 0.5 * x * (1.0 + jnp.tanh(0.7978845608028654 * (x + 0.044715 * (x * x * x))))


def _gmlp_kernel(x_ref, g_ref, wu_ref, wv_ref, ggv_ref, ws_ref, mask_ref, bs_ref,
                 o_ref, gvn_ref, *, n_tiles):
    i = pl.program_id(0)
    h = _rms(x_ref[...], g_ref[...]).astype(bf16)
    u = _gelu_tanh(_dot(h, wu_ref[...]))
    gv = _gelu_tanh(_dot(h, wv_ref[...]))
    mask = mask_ref[...]
    bias = bs_ref[...]
    for g in range(N_GMLP_HEADS):
        cols = slice(g * GMLP_HEAD_DIM, (g + 1) * GMLP_HEAD_DIM)
        gvn = _rms(gv[:, cols], ggv_ref[g:g + 1, :])
        w = (ws_ref[g] * mask).astype(bf16)
        for c in range(PROJ_TM // CHUNK):
            rows = slice(c * CHUNK, (c + 1) * CHUNK)
            mixed = _dot(w, gvn[rows].astype(bf16)) + bias[:, g:g + 1]
            o_ref[rows, cols] = (u[rows, cols] * mixed).astype(bf16)

        @pl.when(i == n_tiles - 1)
        def _():
            gvn_ref[:, cols] = gvn


def _gmlp(x, g, wu, wv, ggv, ws2, mask2, bs2):
    n_tiles = N_ROWS // PROJ_TM
    n_prompt_tiles = N_PROMPT // PROJ_TM
    row = lambda i: (i, 0)
    fixed = lambda i: (0, 0)
    kind3 = lambda i: (i // n_prompt_tiles, 0, 0)
    kind4 = lambda i: (i // n_prompt_tiles, 0, 0, 0)
    return pl.pallas_call(
        functools.partial(_gmlp_kernel, n_tiles=n_tiles),
        grid=(n_tiles,),
        in_specs=[
            pl.BlockSpec((PROJ_TM, D_MODEL), row),
            pl.BlockSpec((1, D_MODEL), fixed),
            pl.BlockSpec((D_MODEL, GMLP_WIDTH), fixed),
            pl.BlockSpec((D_MODEL, GMLP_WIDTH), fixed),
            pl.BlockSpec((N_GMLP_HEADS, GMLP_HEAD_DIM), fixed),
            pl.BlockSpec((None, N_GMLP_HEADS, CHUNK, CHUNK), kind4),
            pl.BlockSpec((None, CHUNK, CHUNK), kind3),
            pl.BlockSpec((None, CHUNK, N_GMLP_HEADS), kind3),
        ],
        out_specs=[pl.BlockSpec((PROJ_TM, GMLP_WIDTH), row),
                   pl.BlockSpec((PROJ_TM, GMLP_WIDTH), fixed)],
        out_shape=[jax.ShapeDtypeStruct((N_ROWS, GMLP_WIDTH), bf16),
                   jax.ShapeDtypeStruct((N_SAMPLE, GMLP_WIDTH), f32)],
        compiler_params=pltpu.CompilerParams(
            dimension_semantics=("arbitrary",), vmem_limit_bytes=VMEM_LIMIT),
        name="gmlp_group",
    )(x, g, wu, wv, ggv, ws2, mask2, bs2)


def _cumsum_kernel(lf_ref, c_ref, ct_ref):
    r = lax.broadcasted_iota(jnp.int32, (CHUNK, CHUNK), 0)
    s = lax.broadcasted_iota(jnp.int32, (CHUNK, CHUNK), 1)
    tri = (s <= r).astype(bf16)

    def body(j, carry):
        rows = pl.ds(pl.multiple_of(j * CHUNK, CHUNK), CHUNK)
        hi, mid, lo = _split3(lf_ref[rows, :])
        c = _dot(tri, hi) + _dot(tri, mid) + _dot(tri, lo) + carry
        c_ref[rows, :] = c
        ct_ref[:, rows] = c.T
        return c[CHUNK - 1:CHUNK, :]

    lax.fori_loop(0, SEQ // CHUNK, body, jnp.zeros((1, LANES), f32))


def _prompt_cumsum(lf_pad):
    return pl.pallas_call(
        _cumsum_kernel,
        grid=(BATCH,),
        in_specs=[pl.BlockSpec((SEQ, LANES), lambda b: (b, 0))],
        out_specs=[pl.BlockSpec((SEQ, LANES), lambda b: (b, 0)),
                   pl.BlockSpec((None, LANES, SEQ), lambda b: (b, 0, 0))],
        out_shape=[jax.ShapeDtypeStruct((N_PROMPT, LANES), f32),
                   jax.ShapeDtypeStruct((BATCH, LANES, SEQ), f32)],
        compiler_params=pltpu.CompilerParams(
            dimension_semantics=("parallel",), vmem_limit_bytes=VMEM_LIMIT),
        name="logf_cumsum",
    )(lf_pad)


def _fox_prompt_kernel(q_ref, k_ref, v_ref, c_ref, ct_ref, o_ref, m_ref, l_ref, acc_ref):
    hp = pl.program_id(1)
    qi = pl.program_id(2)
    tq = ATT_TQ
    lane = lax.broadcasted_iota(jnp.int32, (tq, LANES), 1)
    low = lane < HEAD_DIM
    q2 = q_ref[...]
    zero = jnp.zeros_like(q2)
    q_heads = (jnp.where(low, q2, zero), jnp.where(low, zero, q2))
    cq = c_ref[...]
    cq_heads = tuple(
        jnp.sum(jnp.where(lane == 2 * hp + e, cq, 0.0), axis=1, keepdims=True) for e in range(2))

    m_ref[...] = jnp.full(m_ref.shape, NEG_INF, f32)
    l_ref[...] = jnp.zeros(l_ref.shape, f32)
    acc_ref[...] = jnp.zeros(acc_ref.shape, f32)

    q_pos = qi * tq + lax.broadcasted_iota(jnp.int32, (tq, tq), 0)
    k_off = lax.broadcasted_iota(jnp.int32, (tq, tq), 1)

    def body(ki, _):
        k0 = pl.multiple_of(ki * tq, tq)
        k2 = k_ref[pl.ds(k0, tq), :]
        v2 = v_ref[pl.ds(k0, tq), :]
        causal = (k0 + k_off) <= q_pos
        alphas, pvs = [], []
        for e in range(2):
            ck = ct_ref[pl.ds(2 * hp + e, 1), pl.ds(k0, tq)]
            s = _dot_nt(q_heads[e], k2) + (cq_heads[e] - ck)
            s = jnp.where(causal, s, NEG_INF)
            m_old = m_ref[e]
            m_new = jnp.maximum(m_old, jnp.max(s, axis=1, keepdims=True))
            alpha = jnp.exp(m_old - m_new)
            p = jnp.exp(s - m_new)
            l_ref[e] = alpha * l_ref[e] + jnp.sum(p, axis=1, keepdims=True)
            m_ref[e] = m_new
            alphas.append(alpha)
            pvs.append(_dot(p.astype(bf16), v2))
        acc_ref[...] = (acc_ref[...] * jnp.where(low, alphas[0], alphas[1])
                        + jnp.where(low, pvs[0], pvs[1]))
        return 0

    lax.fori_loop(0, qi + 1, body, 0)
    o_ref[...] = (acc_ref[...] / jnp.where(low, l_ref[0], l_ref[1])).astype(bf16)


def _fox_prompt(qb, kb, vb, c, ct):
    nq = SEQ // ATT_TQ
    return pl.pallas_call(
        _fox_prompt_kernel,
        grid=(BATCH, N_HEADS // 2, nq),
        in_specs=[
            pl.BlockSpec((ATT_TQ, LANES), lambda b, hp, qi: (b * nq + qi, hp)),
            pl.BlockSpec((SEQ, LANES), lambda b, hp, qi: (b, hp)),
            pl.BlockSpec((SEQ, LANES), lambda b, hp, qi: (b, hp)),
            pl.BlockSpec((ATT_TQ, LANES), lambda b, hp, qi: (b * nq + qi, 0)),
            pl.BlockSpec((None, LANES, SEQ), lambda b, hp, qi: (b, 0, 0)),
        ],
        out_specs=pl.BlockSpec((ATT_TQ, LANES), lambda b, hp, qi: (b * nq + qi, hp)),
        out_shape=jax.ShapeDtypeStruct((N_PROMPT, ATTN_WIDTH), bf16),
        scratch_shapes=[pltpu.VMEM((2, ATT_TQ, 1), f32), pltpu.VMEM((2, ATT_TQ, 1), f32),
                        pltpu.VMEM((ATT_TQ, LANES), f32)],
        compiler_params=pltpu.CompilerParams(
            dimension_semantics=("parallel", "parallel", "arbitrary"),
            vmem_limit_bytes=VMEM_LIMIT),
        name="fox_prompt",
    )(qb, kb, vb, c, ct)


def _flat_lane_consts():
    hh = lax.broadcasted_iota(jnp.int32, (N_HEADS, LANES), 0)
    ln = lax.broadcasted_iota(jnp.int32, (N_HEADS, LANES), 1)
    spread = (ln % N_HEADS == hh).astype(bf16)
    rr = lax.broadcasted_iota(jnp.int32, (PAGE_SIZE, LANES), 0)
    l2 = lax.broadcasted_iota(jnp.int32, (PAGE_SIZE, LANES), 1)
    diag = (rr % SUBLANES) == (l2 // N_HEADS)
    return spread, diag


def _suffix_kernel(pt_ref, *refs):
    lf_refs = refs[:SFX_PAGES_PER_STEP]
    out_ref = refs[SFX_PAGES_PER_STEP]
    carry_ref = refs[SFX_PAGES_PER_STEP + 1]
    j = pl.program_id(1)

    @pl.when(j == 0)
    def _():
        carry_ref[...] = jnp.zeros(carry_ref.shape, f32)

    s_i = lax.broadcasted_iota(jnp.int32, (PAGE_SIZE, PAGE_SIZE), 0)
    j_i = lax.broadcasted_iota(jnp.int32, (PAGE_SIZE, PAGE_SIZE), 1)
    after = (j_i > s_i).astype(bf16)
    spread, diag = _flat_lane_consts()

    for i in range(SFX_PAGES_PER_STEP):
        x = lf_refs[i][...]
        hi, mid, lo = _split3(x)
        sfx = _dot(after, hi) + _dot(after, mid) + _dot(after, lo) + carry_ref[...]
        carry_ref[...] += jnp.sum(x, axis=0, keepdims=True)
        hi, mid, lo = _split3(sfx)
        wide = _dot(hi, spread) + _dot(mid, spread) + _dot(lo, spread)
        wide = jnp.where(diag, wide, 0.0)
        flat = jnp.sum(wide.reshape(PAGE_SIZE // SUBLANES, SUBLANES, LANES), axis=1)
        out_ref[SFX_PAGES_PER_STEP - 1 - i] = flat


def _cache_suffix(page_table, cache_logf3):
    g = SFX_PAGES_PER_STEP
    nj = N_PAGES // g

    def lf_map(i):
        return lambda b, j, pt: (pt[b, N_PAGES - 1 - (j * g + i)], 0, 0)

    grid_spec = pltpu.PrefetchScalarGridSpec(
        num_scalar_prefetch=1,
        grid=(DEC_BATCH, nj),
        in_specs=[pl.BlockSpec((None, PAGE_SIZE, N_HEADS), lf_map(i)) for i in range(g)],
        out_specs=pl.BlockSpec((None, g, PAGE_SIZE // SUBLANES, LANES),
                               lambda b, j, pt: (b, nj - 1 - j, 0, 0)),
        scratch_shapes=[pltpu.VMEM((1, N_HEADS), f32)],
    )
    return pl.pallas_call(
        _suffix_kernel,
        grid_spec=grid_spec,
        out_shape=jax.ShapeDtypeStruct((DEC_BATCH, N_PAGES, PAGE_SIZE // SUBLANES, LANES), f32),
        compiler_params=pltpu.CompilerParams(
            dimension_semantics=("parallel", "arbitrary"), vmem_limit_bytes=VMEM_LIMIT),
        name="cache_suffix",
    )(page_table, *([cache_logf3] * g))


def _merge_heads(s_all, width):
    head_of_lane = lax.broadcasted_iota(jnp.int32, (SUBLANES, width), 1) % N_HEADS
    out = s_all[0:SUBLANES]
    for hh in range(1, N_HEADS):
        out = jnp.where(head_of_lane == hh, s_all[hh * SUBLANES:(hh + 1) * SUBLANES], out)
    return out


def _expand_heads(p, width):
    head_of_lane = lax.broadcasted_iota(jnp.int32, (SUBLANES, width), 1) % N_HEADS
    return jnp.concatenate(
        [jnp.where(head_of_lane == hh, p, 0.0) for hh in range(N_HEADS)], axis=0)


def _over_keys(x, op):
    for shift in (N_HEADS, 2 * N_HEADS, 4 * N_HEADS):
        x = op(x, pltpu.roll(x, shift, axis=1))
    return x


def _rows_from_lanes(stat):
    lane = lax.broadcasted_iota(jnp.int32, (SUBLANES, LANES), 1)
    return jnp.concatenate(
        [jnp.sum(jnp.where(lane == hh, stat, 0.0), axis=1, keepdims=True) for hh in range(N_HEADS)],
        axis=0)


def _fox_sample_kernel(pt_ref, q_ref, kn_ref, vn_ref, lfn_ref, sfx_ref, *refs):
    g = PAGES_PER_STEP
    k_refs = refs[:g]
    v_refs = refs[g:2 * g]
    o_ref, m_ref, l_ref, acc_ref, cn_ref = refs[2 * g:]
    j = pl.program_id(1)
    q = q_ref[...]

    @pl.when(j == 0)
    def _():
        lf = lfn_ref[...]
        row = lax.broadcasted_iota(jnp.int32, (SUBLANES, LANES), 0)
        c = lf
        for sh in (1, 2, 4):
            c = c + jnp.where(row >= sh, pltpu.roll(c, sh, axis=0), 0.0)
        hh = lax.broadcasted_iota(jnp.int32, (LANES, LANES), 0)
        ln = lax.broadcasted_iota(jnp.int32, (LANES, LANES), 1)
        spread = ((ln % N_HEADS == hh) & (hh < N_HEADS)).astype(bf16)
        hi, mid, lo = _split3(c)
        cn = _dot(hi, spread) + _dot(mid, spread) + _dot(lo, spread)
        cn_ref[...] = cn
        key = lax.broadcasted_iota(jnp.int32, (SUBLANES, LANES), 1) // N_HEADS
        c_key = jnp.sum(jnp.where(key == row, cn, 0.0), axis=0, keepdims=True)
        s = _merge_heads(_dot_nt(q, kn_ref[...]), LANES) + (cn - c_key)
        s = jnp.where(key <= row, s, NEG_INF)
        m = _over_keys(s, jnp.maximum)
        p = jnp.exp(s - m)
        m_ref[...] = m
        l_ref[...] = _over_keys(p, jnp.add)
        acc_ref[...] = _dot(_expand_heads(p, LANES), vn_ref[...])

    cn = cn_ref[...]
    n_tiles = PAGE_ROWS // LANES
    for pg in range(g):
        k_ref, v_ref = k_refs[pg], v_refs[pg]
        tiles = []
        for c in range(PAGE_ROWS // MXU_DIM):
            s_all = _dot_nt(q, k_ref[c * MXU_DIM:(c + 1) * MXU_DIM, :])
            tiles.append(_merge_heads(s_all, MXU_DIM))
        s = jnp.concatenate(tiles, axis=1)
        sfx = sfx_ref[pg]
        bias = jnp.concatenate([cn + sfx[t:t + 1, :] for t in range(n_tiles)], axis=1)
        s = s + bias
        m_pg = s[:, 0:LANES]
        for t in range(1, n_tiles):
            m_pg = jnp.maximum(m_pg, s[:, t * LANES:(t + 1) * LANES])
        m_old = m_ref[...]
        m_new = jnp.maximum(m_old, _over_keys(m_pg, jnp.maximum))
        alpha = jnp.exp(m_old - m_new)
        p = jnp.exp(s - jnp.concatenate([m_new] * n_tiles, axis=1))
        l_pg = p[:, 0:LANES]
        for t in range(1, n_tiles):
            l_pg = l_pg + p[:, t * LANES:(t + 1) * LANES]
        l_ref[...] = alpha * l_ref[...] + _over_keys(l_pg, jnp.add)
        m_ref[...] = m_new
        pv = _dot(_expand_heads(p, PAGE_ROWS), v_ref[...])
        acc_ref[...] = acc_ref[...] * _rows_from_lanes(alpha) + pv

    @pl.when(j == pl.num_programs(1) - 1)
    def _():
        o_ref[...] = acc_ref[...] / _rows_from_lanes(l_ref[...])


def _fox_sample(page_table, q_rows, k_new, v_new, lf_pad, sfx, cache_k3, cache_v3):
    g = PAGES_PER_STEP
    nj = N_PAGES // g
    first_sample_blk = N_PROMPT // DEC_SEQ

    def page_map(i):
        return lambda b, j, pt: (pt[b, j * g + i], 0, 0)

    per_b = lambda b, j, pt: (b, 0, 0)
    page_spec = lambda i: pl.BlockSpec((None, PAGE_ROWS, HEAD_DIM), page_map(i))
    rows = DEC_SEQ * N_HEADS
    grid_spec = pltpu.PrefetchScalarGridSpec(
        num_scalar_prefetch=1,
        grid=(DEC_BATCH, nj),
        in_specs=[
            pl.BlockSpec((None, rows, HEAD_DIM), per_b),
            pl.BlockSpec((None, rows, HEAD_DIM), per_b),
            pl.BlockSpec((None, rows, HEAD_DIM), per_b),
            pl.BlockSpec((DEC_SEQ, LANES), lambda b, j, pt: (first_sample_blk + b, 0)),
            pl.BlockSpec((None, g, PAGE_SIZE // SUBLANES, LANES), lambda b, j, pt: (b, j, 0, 0)),
        ] + [page_spec(i) for i in range(g)] + [page_spec(i) for i in range(g)],
        out_specs=pl.BlockSpec((None, rows, HEAD_DIM), per_b),
        scratch_shapes=[pltpu.VMEM((SUBLANES, LANES), f32), pltpu.VMEM((SUBLANES, LANES), f32),
                        pltpu.VMEM((rows, HEAD_DIM), f32), pltpu.VMEM((SUBLANES, LANES), f32)],
    )
    return pl.pallas_call(
        _fox_sample_kernel,
        grid_spec=grid_spec,
        out_shape=jax.ShapeDtypeStruct((DEC_BATCH, rows, HEAD_DIM), f32),
        compiler_params=pltpu.CompilerParams(
            dimension_semantics=("parallel", "arbitrary"), vmem_limit_bytes=VMEM_LIMIT),
        name="fox_sample",
    )(page_table, q_rows, k_new, v_new, lf_pad, sfx, *([cache_k3] * g), *([cache_v3] * g))


def _oproj_kernel(x_ref, a_ref, m_ref, wa_ref, wm_ref, o_ref):
    o_ref[...] = x_ref[...] + _dot(a_ref[...], wa_ref[...]) + _dot(m_ref[...], wm_ref[...])


def _out_proj(x, att, gm, wa, wm):
    row = lambda i: (i, 0)
    fixed = lambda i: (0, 0)
    return pl.pallas_call(
        _oproj_kernel,
        grid=(N_ROWS // PROJ_TM,),
        in_specs=[
            pl.BlockSpec((PROJ_TM, D_MODEL), row),
            pl.BlockSpec((PROJ_TM, ATTN_WIDTH), row),
            pl.BlockSpec((PROJ_TM, GMLP_WIDTH), row),
            pl.BlockSpec((ATTN_WIDTH, D_MODEL), fixed),
            pl.BlockSpec((GMLP_WIDTH, D_MODEL), fixed),
        ],
        out_specs=pl.BlockSpec((PROJ_TM, D_MODEL), row),
        out_shape=jax.ShapeDtypeStruct((N_ROWS, D_MODEL), f32),
        compiler_params=pltpu.CompilerParams(
            dimension_semantics=("parallel",), vmem_limit_bytes=VMEM_LIMIT),
        name="out_proj",
    )(x, att, gm, wa, wm)


def _pad_ff(w, axis):
    pad = [(0, 0), (0, 0)]
    pad[axis] = (0, D_FF_PAD - D_FF)
    return jnp.pad(w.astype(bf16), pad)


def kernel(x_prompt, x_sample, cache_k, cache_v, cache_logf, page_table, g_ffn1, w1_gate, w1_up, w1_down, g_mix, w_in, b_f, g_gv, w_s, b_s, w_out, g_ffn2, w2_gate, w2_up, w2_down, g_final):
    l = 0
    x = jnp.concatenate([x_prompt.reshape(N_PROMPT, D_MODEL), x_sample.reshape(N_SAMPLE, D_MODEL)], axis=0)
    g_fin = g_final.reshape(1, D_MODEL)

    w1g, w1u, w1d = _pad_ff(w1_gate[l], 1), _pad_ff(w1_up[l], 1), _pad_ff(w1_down[l], 0)
    w2g, w2u, w2d = _pad_ff(w2_gate[l], 1), _pad_ff(w2_up[l], 1), _pad_ff(w2_down[l], 0)
    win = w_in[l]
    a = ATTN_WIDTH
    wq = win[:, 0:a].astype(bf16)
    wk = win[:, a:2 * a].astype(bf16)
    wv = win[:, 2 * a:3 * a].astype(bf16)
    wf = jnp.pad(win[:, 3 * a:3 * a + N_HEADS], ((0, 0), (0, LANES - N_HEADS))).astype(bf16)
    wu = win[:, 3 * a + N_HEADS:3 * a + N_HEADS + GMLP_WIDTH].astype(bf16)
    wgv = win[:, 3 * a + N_HEADS + GMLP_WIDTH:].astype(bf16)
    bfp = jnp.pad(b_f[l].reshape(1, N_HEADS), ((0, 0), (0, LANES - N_HEADS)))
    wo = w_out[l].astype(bf16)

    reps = CHUNK // DEC_SEQ
    ws2 = jnp.stack([w_s[l], jnp.tile(w_s[l][:, :DEC_SEQ, :DEC_SEQ], (1, reps, reps))])
    ti = jnp.arange(CHUNK)
    tril = (ti[None, :] <= ti[:, None])
    same = (ti[None, :] // DEC_SEQ) == (ti[:, None] // DEC_SEQ)
    mask2 = jnp.stack([tril, tril & same]).astype(f32)
    bs2 = jnp.stack([b_s[l].T, jnp.tile(b_s[l][:, :DEC_SEQ], (1, reps)).T])

    x1 = _ffn(x, g_ffn1[l].reshape(1, D_MODEL), w1g, w1u, w1d, g_fin, False)

    gm = g_mix[l].reshape(1, D_MODEL)
    q32, qb, k32, kb, v32, vb, lf_pad = _attn_proj(x1, gm, wq, wk, wv, wf, bfp)
    gmlp_out, gvn_s = _gmlp(x1, gm, wu, wgv, g_gv[l], ws2, mask2, bs2)

    c, ct = _prompt_cumsum(lf_pad)
    att_p = _fox_prompt(qb, kb, vb, c, ct)

    k_s = k32[N_PROMPT:].reshape(DEC_BATCH, DEC_SEQ, N_HEADS, HEAD_DIM)
    v_s = v32[N_PROMPT:].reshape(DEC_BATCH, DEC_SEQ, N_HEADS, HEAD_DIM)
    q_rows = (q32[N_PROMPT:].reshape(DEC_BATCH, DEC_SEQ, N_HEADS, HEAD_DIM)
              .transpose(0, 2, 1, 3).reshape(DEC_BATCH, N_HEADS * DEC_SEQ, HEAD_DIM))
    rows = DEC_SEQ * N_HEADS
    sfx = _cache_suffix(page_table, cache_logf[l])
    att_s = _fox_sample(page_table, q_rows, k_s.reshape(DEC_BATCH, rows, HEAD_DIM),
                        v_s.reshape(DEC_BATCH, rows, HEAD_DIM), lf_pad, sfx,
                        cache_k[l].reshape(-1, PAGE_ROWS, HEAD_DIM),
                        cache_v[l].reshape(-1, PAGE_ROWS, HEAD_DIM))
    att_s = (att_s.reshape(DEC_BATCH, N_HEADS, DEC_SEQ, HEAD_DIM).transpose(0, 2, 1, 3)
             .reshape(N_SAMPLE, ATTN_WIDTH).astype(bf16))

    att = jnp.concatenate([att_p, att_s], axis=0)
    x2 = _out_proj(x1, att, gmlp_out, wo[:ATTN_WIDTH], wo[ATTN_WIDTH:])
    y = _ffn(x2, g_ffn2[l].reshape(1, D_MODEL), w2g, w2u, w2d, g_fin, True)

    lf = lf_pad[:, :N_HEADS]
    shp = (1, BATCH, SEQ, N_HEADS, HEAD_DIM)
    return (y[:N_PROMPT].reshape(BATCH, SEQ, D_MODEL),
            y[N_PROMPT:].reshape(DEC_BATCH, DEC_SEQ, D_MODEL),
            k32[:N_PROMPT].reshape(shp),
            v32[:N_PROMPT].reshape(shp),
            lf[:N_PROMPT].reshape(1, BATCH, SEQ, N_HEADS),
            k_s[None], v_s[None],
            lf[N_PROMPT:].reshape(1, DEC_BATCH, DEC_SEQ, N_HEADS),
            gvn_s.reshape(1, DEC_BATCH, DEC_SEQ, N_GMLP_HEADS, GMLP_HEAD_DIM))
```

```python
import functools

import jax
import jax.numpy as jnp
from jax import lax
from jax.experimental import pallas as pl
from jax.experimental.pallas import tpu as pltpu

f32 = jnp.float32
bf16 = jnp.bfloat16

D_MODEL = 2048
BATCH = 2
SEQ = 4096
DEC_BATCH = 32
DEC_SEQ = 8
PAST_LEN = 16384
PAGE_SIZE = 128
ATTN_WIDTH = 1024
GMLP_WIDTH = 1024
HEAD_DIM = 64
N_HEADS = 16
CHUNK = 128
GMLP_HEAD_DIM = 128
N_GMLP_HEADS = 8
D_FF = 5504
EPS = 1e-6
NEG_INF = -1e30
N_PAGES = PAST_LEN // PAGE_SIZE

LANES = 128
SUBLANES = 8
MXU_DIM = 256

N_PROMPT = BATCH * SEQ
N_SAMPLE = DEC_BATCH * DEC_SEQ
N_ROWS = N_PROMPT + N_SAMPLE
D_FF_PAD = -(-D_FF // (2 * MXU_DIM)) * (2 * MXU_DIM)

FFN_TM = 768
FFN_TF = 512
PROJ_TM = 256
ATT_T = 512
PAGES_PER_STEP = 8
SFX_PAGES_PER_STEP = 16
VMEM_LIMIT = 56 * 1024 * 1024


def _rms(x, g):
    return x * lax.rsqrt(jnp.mean(x * x, axis=-1, keepdims=True) + EPS) * g


def _split3(x):
    hi = x.astype(bf16)
    r1 = x - hi.astype(f32)
    mid = r1.astype(bf16)
    lo = (r1 - mid.astype(f32)).astype(bf16)
    return hi, mid, lo


def _dot(a, b):
    return jnp.dot(a, b, preferred_element_type=f32)


def _dot_nt(a, b):
    return lax.dot_general(a, b, (((1,), (1,)), ((), ())), preferred_element_type=f32)


def _dot3(x, sel):
    hi, mid, lo = _split3(x)
    return _dot(hi, sel) + _dot(mid, sel) + _dot(lo, sel)


def _ffn_kernel(x_ref, g_ref, wg_ref, wu_ref, wd_ref, gfin_ref, o_ref, h_ref, *, final_norm, nf):
    f = pl.program_id(1)

    @pl.when(f == 0)
    def _():
        x = x_ref[...]
        h_ref[...] = _rms(x, g_ref[...]).astype(bf16)
        o_ref[...] = x

    h = h_ref[...]
    gate = _dot(h, wg_ref[...])
    up = _dot(h, wu_ref[...])
    act = (gate / (1.0 + jnp.exp(-gate))) * up * 0.5
    o_ref[...] += _dot(act.astype(bf16), wd_ref[...])

    if final_norm:
        @pl.when(f == nf - 1)
        def _():
            o_ref[...] = _rms(o_ref[...], gfin_ref[...])


def _ffn(x, g, wg, wu, wd, gfin, final_norm):
    nf = D_FF_PAD // FFN_TF
    kern = functools.partial(_ffn_kernel, final_norm=final_norm, nf=nf)
    return pl.pallas_call(
        kern,
        grid=(N_ROWS // FFN_TM, nf),
        in_specs=[
            pl.BlockSpec((FFN_TM, D_MODEL), lambda i, f: (i, 0)),
            pl.BlockSpec((1, D_MODEL), lambda i, f: (0, 0)),
            pl.BlockSpec((D_MODEL, FFN_TF), lambda i, f: (0, f)),
            pl.BlockSpec((D_MODEL, FFN_TF), lambda i, f: (0, f)),
            pl.BlockSpec((FFN_TF, D_MODEL), lambda i, f: (f, 0)),
            pl.BlockSpec((1, D_MODEL), lambda i, f: (0, 0)),
        ],
        out_specs=pl.BlockSpec((FFN_TM, D_MODEL), lambda i, f: (i, 0)),
        out_shape=jax.ShapeDtypeStruct((N_ROWS, D_MODEL), f32),
        scratch_shapes=[pltpu.VMEM((FFN_TM, D_MODEL), bf16)],
        compiler_params=pltpu.CompilerParams(
            dimension_semantics=("parallel", "arbitrary"), vmem_limit_bytes=VMEM_LIMIT),
        name="ffn_half",
    )(x, g, wg, wu, wd, gfin)


def _aproj_kernel(x_ref, g_ref, wq_ref, wk_ref, wv_ref, wvt_ref, wf_ref, bf_ref,
                  q_ref, qb_ref, k_ref, kb_ref, v_ref, vt_ref, lf_ref):
    h = _rms(x_ref[...], g_ref[...]).astype(bf16)
    q = _dot(h, wq_ref[...]) * (HEAD_DIM ** -0.5)
    q_ref[...] = q
    qb_ref[...] = q.astype(bf16)
    k = _dot(h, wk_ref[...])
    k_ref[...] = k
    kb_ref[...] = k.astype(bf16)
    v_ref[...] = _dot(h, wv_ref[...])
    vt_ref[...] = _dot_nt(wvt_ref[...], h).astype(bf16)
    z = _dot(h, wf_ref[...]) + bf_ref[...]
    lf_ref[...] = -(jnp.maximum(-z, 0.0) + jnp.log1p(jnp.exp(-jnp.abs(z))))


def _attn_proj(x, g, wq, wk, wv, wvt, wf, bfp):
    row = lambda i: (i, 0)
    fixed = lambda i: (0, 0)
    wide = lambda dt: jax.ShapeDtypeStruct((N_ROWS, ATTN_WIDTH), dt)
    return pl.pallas_call(
        _aproj_kernel,
        grid=(N_ROWS // PROJ_TM,),
        in_specs=[
            pl.BlockSpec((PROJ_TM, D_MODEL), row),
            pl.BlockSpec((1, D_MODEL), fixed),
            pl.BlockSpec((D_MODEL, ATTN_WIDTH), fixed),
            pl.BlockSpec((D_MODEL, ATTN_WIDTH), fixed),
            pl.BlockSpec((D_MODEL, ATTN_WIDTH), fixed),
            pl.BlockSpec((ATTN_WIDTH, D_MODEL), fixed),
            pl.BlockSpec((D_MODEL, LANES), fixed),
            pl.BlockSpec((1, LANES), fixed),
        ],
        out_specs=[pl.BlockSpec((PROJ_TM, ATTN_WIDTH), row)] * 5
        + [pl.BlockSpec((ATTN_WIDTH, PROJ_TM), lambda i: (0, i)), pl.BlockSpec((PROJ_TM, LANES), row)],
        out_shape=[wide(f32), wide(bf16), wide(f32), wide(bf16), wide(f32),
                   jax.ShapeDtypeStruct((ATTN_WIDTH, N_ROWS), bf16),
                   jax.ShapeDtypeStruct((N_ROWS, LANES), f32)],
        compiler_params=pltpu.CompilerParams(
            dimension_semantics=("parallel",), vmem_limit_bytes=VMEM_LIMIT),
        name="attn_proj",
    )(x, g, wq, wk, wv, wvt, wf, bfp)


def _gelu_tanh(x):
    return 0.5 * x * (1.0 + jnp.tanh(0.7978845608028654 * (x + 0.044715 * (x * x * x))))


def _gmlp_kernel(x_ref, g_ref, wu_ref, wv_ref, ggv_ref, ws_ref, mask_ref, bs_ref,
                 o_ref, gvn_ref, *, n_tiles):
    i = pl.program_id(0)
    h = _rms(x_ref[...], g_ref[...]).astype(bf16)
    u = _gelu_tanh(_dot(h, wu_ref[...]))
    gv = _gelu_tanh(_dot(h, wv_ref[...]))
    mask = mask_ref[...]
    bias = bs_ref[...]
    for g in range(N_GMLP_HEADS):
        cols = slice(g * GMLP_HEAD_DIM, (g + 1) * GMLP_HEAD_DIM)
        gvn = _rms(gv[:, cols], ggv_ref[g:g + 1, :])
        w = (ws_ref[g] * mask).astype(bf16)
        for c in range(PROJ_TM // CHUNK):
            rows = slice(c * CHUNK, (c + 1) * CHUNK)
            mixed = _dot(w, gvn[rows].astype(bf16)) + bias[:, g:g + 1]
            o_ref[rows, cols] = (u[rows, cols] * mixed).astype(bf16)

        @pl.when(i == n_tiles - 1)
        def _():
            gvn_ref[:, cols] = gvn


def _gmlp(x, g, wu, wv, ggv, ws2, mask2, bs2):
    n_tiles = N_ROWS // PROJ_TM
    n_prompt_tiles = N_PROMPT // PROJ_TM
    row = lambda i: (i, 0)
    fixed = lambda i: (0, 0)
    kind3 = lambda i: (i // n_prompt_tiles, 0, 0)
    kind4 = lambda i: (i // n_prompt_tiles, 0, 0, 0)
    return pl.pallas_call(
        functools.partial(_gmlp_kernel, n_tiles=n_tiles),
        grid=(n_tiles,),
        in_specs=[
            pl.BlockSpec((PROJ_TM, D_MODEL), row),
            pl.BlockSpec((1, D_MODEL), fixed),
            pl.BlockSpec((D_MODEL, GMLP_WIDTH), fixed),
            pl.BlockSpec((D_MODEL, GMLP_WIDTH), fixed),
            pl.BlockSpec((N_GMLP_HEADS, GMLP_HEAD_DIM), fixed),
            pl.BlockSpec((None, N_GMLP_HEADS, CHUNK, CHUNK), kind4),
            pl.BlockSpec((None, CHUNK, CHUNK), kind3),
            pl.BlockSpec((None, CHUNK, N_GMLP_HEADS), kind3),
        ],
        out_specs=[pl.BlockSpec((PROJ_TM, GMLP_WIDTH), row),
                   pl.BlockSpec((PROJ_TM, GMLP_WIDTH), fixed)],
        out_shape=[jax.ShapeDtypeStruct((N_ROWS, GMLP_WIDTH), bf16),
                   jax.ShapeDtypeStruct((N_SAMPLE, GMLP_WIDTH), f32)],
        compiler_params=pltpu.CompilerParams(
            dimension_semantics=("arbitrary",), vmem_limit_bytes=VMEM_LIMIT),
        name="gmlp_group",
    )(x, g, wu, wv, ggv, ws2, mask2, bs2)


def _cumsum_kernel(lf_ref, c_ref):
    r = lax.broadcasted_iota(jnp.int32, (CHUNK, CHUNK), 0)
    s = lax.broadcasted_iota(jnp.int32, (CHUNK, CHUNK), 1)
    tri = (s <= r).astype(bf16)

    def body(j, carry):
        rows = pl.ds(pl.multiple_of(j * CHUNK, CHUNK), CHUNK)
        hi, mid, lo = _split3(lf_ref[rows, :])
        c = _dot(tri, hi) + _dot(tri, mid) + _dot(tri, lo) + carry
        c_ref[rows, :] = c
        return c[CHUNK - 1:CHUNK, :]

    lax.fori_loop(0, SEQ // CHUNK, body, jnp.zeros((1, LANES), f32))


def _prompt_cumsum(lf_pad):
    return pl.pallas_call(
        _cumsum_kernel,
        grid=(BATCH,),
        in_specs=[pl.BlockSpec((SEQ, LANES), lambda b: (b, 0))],
        out_specs=pl.BlockSpec((SEQ, LANES), lambda b: (b, 0)),
        out_shape=jax.ShapeDtypeStruct((N_PROMPT, LANES), f32),
        compiler_params=pltpu.CompilerParams(
            dimension_semantics=("parallel",), vmem_limit_bytes=VMEM_LIMIT),
        name="logf_cumsum",
    )(lf_pad)


N_SPLIT = 3


def _bias_lanes(e):
    return HEAD_DIM * (1 - e)


def _fox_prompt_kernel(q_ref, k_ref, vt_ref, c_ref, o_ref,
                       kp_ref, g_ref, qp_ref, sa_ref, sb_ref, m_ref, l_ref, acc_ref):
    hp = pl.program_id(1)
    qi = pl.program_id(2)
    t = ATT_T
    lane = lax.broadcasted_iota(jnp.int32, (t, LANES), 1)

    def in_span(e, first, n):
        base = _bias_lanes(e) + first
        return (lane >= base) & (lane < base + n)

    @pl.when(qi == 0)
    def _():
        r = lax.broadcasted_iota(jnp.int32, (LANES, LANES), 0)
        l2 = lax.broadcasted_iota(jnp.int32, (LANES, LANES), 1)
        sels = []
        for j in range(N_SPLIT):
            sel = jnp.zeros((LANES, LANES), f32)
            for e in range(2):
                src = r == 2 * hp + e
                sel = sel - (src & (l2 == _bias_lanes(e) + j)).astype(f32)
                sel = sel + (src & (l2 == _bias_lanes(e) + N_SPLIT + j)).astype(f32)
            sels.append(sel.astype(bf16))

        def prep(i, _):
            rows = pl.ds(pl.multiple_of(i * t, t), t)
            parts = _split3(c_ref[rows, :])
            g = sum(_dot(p, s) for p, s in zip(parts, sels))
            g_ref[rows, :] = g
            k2 = k_ref[rows, :].astype(f32)
            for e in range(2):
                own = (lane < HEAD_DIM) if e == 0 else (lane >= HEAD_DIM)
                extra = jnp.where(in_span(e, 0, N_SPLIT), g,
                                  jnp.where(in_span(e, N_SPLIT, N_SPLIT), 1.0, 0.0))
                kp_ref[e, rows, :] = jnp.where(own, k2, extra).astype(bf16)
            return 0

        lax.fori_loop(0, SEQ // t, prep, 0)

    q0 = pl.multiple_of(qi * t, t)
    q2 = q_ref[...].astype(f32)
    gq = g_ref[pl.ds(q0, t), :]
    for e in range(2):
        own = (lane < HEAD_DIM) if e == 0 else (lane >= HEAD_DIM)
        extra = jnp.where(in_span(e, 0, N_SPLIT), 1.0,
                          jnp.where(in_span(e, N_SPLIT, N_SPLIT), gq, 0.0))
        qp_ref[e] = jnp.where(own, q2, extra).astype(bf16)

    m_ref[...] = jnp.full(m_ref.shape, NEG_INF, f32)
    l_ref[...] = jnp.zeros(l_ref.shape, f32)
    acc_ref[...] = jnp.zeros(acc_ref.shape, f32)

    def scores(blk, s_ref):
        k0 = pl.multiple_of(blk * t, t)
        for e in range(2):
            s_ref[e] = _dot_nt(kp_ref[e, pl.ds(k0, t), :], qp_ref[e])

    def consume(blk, s_ref, on_diagonal):
        k0 = pl.multiple_of(blk * t, t)
        for e in range(2):
            st = s_ref[e]
            if on_diagonal:
                key = lax.broadcasted_iota(jnp.int32, (t, t), 0)
                qry = lax.broadcasted_iota(jnp.int32, (t, t), 1)
                st = jnp.where(key <= qry, st, NEG_INF)
            m_old = m_ref[e]
            m_new = jnp.maximum(m_old, jnp.max(st, axis=0, keepdims=True))
            alpha = jnp.exp(m_old - m_new)
            pt = jnp.exp(st - m_new)
            l_ref[e] = alpha * l_ref[e] + jnp.sum(pt, axis=0, keepdims=True)
            m_ref[e] = m_new
            vt = vt_ref[e * HEAD_DIM:(e + 1) * HEAD_DIM, pl.ds(k0, t)]
            acc_ref[e] = acc_ref[e] * alpha + _dot(vt, pt.astype(bf16))

    scores(0, sa_ref)

    def pair(p, _):
        j = 2 * p
        scores(j + 1, sb_ref)
        consume(j, sa_ref, False)
        scores(j + 2, sa_ref)
        consume(j + 1, sb_ref, False)
        return 0

    lax.fori_loop(0, qi // 2, pair, 0)

    @pl.when(qi % 2 == 0)
    def _():
        consume(qi, sa_ref, True)

    @pl.when(qi % 2 == 1)
    def _():
        scores(qi, sb_ref)
        consume(qi - 1, sa_ref, False)
        consume(qi, sb_ref, True)

    for e in range(2):
        o_ref[e * HEAD_DIM:(e + 1) * HEAD_DIM, :] = (acc_ref[e] / l_ref[e]).astype(bf16)


def _fox_prompt(qb, kb, vt, c):
    nq = SEQ // ATT_T
    return pl.pallas_call(
        _fox_prompt_kernel,
        grid=(BATCH, N_HEADS // 2, nq),
        in_specs=[
            pl.BlockSpec((ATT_T, LANES), lambda b, hp, qi: (b * nq + qi, hp)),
            pl.BlockSpec((SEQ, LANES), lambda b, hp, qi: (b, hp)),
            pl.BlockSpec((LANES, SEQ), lambda b, hp, qi: (hp, b)),
            pl.BlockSpec((SEQ, LANES), lambda b, hp, qi: (b, 0)),
        ],
        out_specs=pl.BlockSpec((LANES, ATT_T), lambda b, hp, qi: (hp, b * nq + qi)),
        out_shape=jax.ShapeDtypeStruct((ATTN_WIDTH, N_PROMPT), bf16),
        scratch_shapes=[pltpu.VMEM((2, SEQ, LANES), bf16), pltpu.VMEM((SEQ, LANES), f32),
                        pltpu.VMEM((2, ATT_T, LANES), bf16),
                        pltpu.VMEM((2, ATT_T, ATT_T), f32), pltpu.VMEM((2, ATT_T, ATT_T), f32),
                        pltpu.VMEM((2, 1, ATT_T), f32), pltpu.VMEM((2, 1, ATT_T), f32),
                        pltpu.VMEM((2, HEAD_DIM, ATT_T), f32)],
        compiler_params=pltpu.CompilerParams(
            dimension_semantics=("parallel", "parallel", "arbitrary"),
            vmem_limit_bytes=VMEM_LIMIT),
        name="fox_prompt",
    )(qb, kb, vt, c)


def _suffix_kernel(pt_ref, *refs):
    g = SFX_PAGES_PER_STEP
    lf_refs = refs[:g]
    out_ref, carry_ref = refs[g:]
    j = pl.program_id(1)

    @pl.when(j == 0)
    def _():
        carry_ref[...] = jnp.zeros(carry_ref.shape, f32)

    j_i = lax.broadcasted_iota(jnp.int32, (PAGE_SIZE, PAGE_SIZE), 0)
    s_i = lax.broadcasted_iota(jnp.int32, (PAGE_SIZE, PAGE_SIZE), 1)
    after = (j_i > s_i).astype(bf16)

    x = jnp.concatenate([lf_refs[i][...] for i in range(g)], axis=0)
    within = _dot3(x, after)
    for i in range(g):
        rows = slice(i * N_HEADS, (i + 1) * N_HEADS)
        carry = carry_ref[...]
        out_ref[g - 1 - i] = within[rows] + carry
        carry_ref[...] = carry + jnp.sum(x[rows], axis=1, keepdims=True)


def _cache_suffix(page_table, cache_lft):
    g = SFX_PAGES_PER_STEP
    nj = N_PAGES // g

    def lf_map(i):
        return lambda b, j, pt: (pt[b, N_PAGES - 1 - (j * g + i)], 0, 0)

    grid_spec = pltpu.PrefetchScalarGridSpec(
        num_scalar_prefetch=1,
        grid=(DEC_BATCH, nj),
        in_specs=[pl.BlockSpec((None, N_HEADS, PAGE_SIZE), lf_map(i)) for i in range(g)],
        out_specs=pl.BlockSpec((None, g, N_HEADS, PAGE_SIZE), lambda b, j, pt: (b, nj - 1 - j, 0, 0)),
        scratch_shapes=[pltpu.VMEM((N_HEADS, 1), f32)],
    )
    return pl.pallas_call(
        _suffix_kernel,
        grid_spec=grid_spec,
        out_shape=jax.ShapeDtypeStruct((DEC_BATCH, N_PAGES, N_HEADS, PAGE_SIZE), f32),
        compiler_params=pltpu.CompilerParams(
            dimension_semantics=("parallel", "arbitrary"), vmem_limit_bytes=VMEM_LIMIT),
        name="cache_suffix",
    )(page_table, *([cache_lft] * g))


def _fox_sample_kernel(pt_ref, q_ref, kn_ref, vn_ref, lfn_ref, sfx_ref, *refs):
    g = PAGES_PER_STEP
    k_refs = refs[:g]
    v_refs = refs[g:2 * g]
    o_ref, m_ref, l_ref, acc_ref, cn_ref = refs[2 * g:]
    j = pl.program_id(1)
    tok = lax.broadcasted_iota(jnp.int32, (DEC_SEQ, PAGE_SIZE), 0)
    key = lax.broadcasted_iota(jnp.int32, (DEC_SEQ, PAGE_SIZE), 1)

    @pl.when(j == 0)
    def _():
        j_i = lax.broadcasted_iota(jnp.int32, (PAGE_SIZE, PAGE_SIZE), 0)
        s_i = lax.broadcasted_iota(jnp.int32, (PAGE_SIZE, PAGE_SIZE), 1)
        upto = (j_i <= s_i).astype(bf16)
        c_new = _dot3(lfn_ref[...], upto)

        for h in range(N_HEADS):
            c_row = c_new[h:h + 1, :]
            c_col = jnp.sum(jnp.where(key == tok, c_row, 0.0), axis=1, keepdims=True)
            c_b = jnp.broadcast_to(c_col, (DEC_SEQ, PAGE_SIZE))
            cn_ref[h] = c_b
            s = _dot(q_ref[h], kn_ref[h]) + (c_b - c_row)
            s = jnp.where(key <= tok, s, NEG_INF)
            m = jnp.max(s, axis=1, keepdims=True)
            p = jnp.exp(s - m)
            m_ref[h] = jnp.broadcast_to(m, (DEC_SEQ, PAGE_SIZE))
            l_ref[h] = p
            acc_ref[h] = _dot_nt(p, vn_ref[h])

    scores, row_max = [], []
    for h in range(N_HEADS):
        q = q_ref[h]
        bias = cn_ref[h]
        s = [_dot(q, k_refs[pg][h]) + (bias + sfx_ref[pg, h:h + 1, :]) for pg in range(g)]
        mx = s[0]
        for pg in range(1, g):
            mx = jnp.maximum(mx, s[pg])
        scores.append(s)
        row_max.append(jnp.max(mx, axis=1, keepdims=True))
    for h in range(N_HEADS):
        s = scores[h]
        m_old = m_ref[h]
        m_new = jnp.maximum(m_old, row_max[h])
        alpha = jnp.exp(m_old - m_new)
        p = [jnp.exp(s[pg] - m_new) for pg in range(g)]
        l_ref[h] = alpha * l_ref[h] + sum(p)
        m_ref[h] = m_new
        pv = sum(_dot_nt(p[pg], v_refs[pg][h]) for pg in range(g))
        acc_ref[h] = acc_ref[h] * alpha[:, :HEAD_DIM] + pv

    @pl.when(j == pl.num_programs(1) - 1)
    def _():
        for h in range(N_HEADS):
            o_ref[h] = acc_ref[h] / jnp.sum(l_ref[h], axis=1, keepdims=True)


def _fox_sample(page_table, q4, kn_t, vn_t, lfn_t, sfx, cache_kt, cache_vt):
    g = PAGES_PER_STEP
    nj = N_PAGES // g

    def page_map(i):
        return lambda b, j, pt: (pt[b, j * g + i], 0, 0, 0)

    per_b4 = lambda b, j, pt: (b, 0, 0, 0)
    page_spec = lambda i: pl.BlockSpec((None, N_HEADS, HEAD_DIM, PAGE_SIZE), page_map(i))
    state = pltpu.VMEM((N_HEADS, DEC_SEQ, PAGE_SIZE), f32)
    grid_spec = pltpu.PrefetchScalarGridSpec(
        num_scalar_prefetch=1,
        grid=(DEC_BATCH, nj),
        in_specs=[
            pl.BlockSpec((None, N_HEADS, DEC_SEQ, HEAD_DIM), per_b4),
            pl.BlockSpec((None, N_HEADS, HEAD_DIM, PAGE_SIZE), per_b4),
            pl.BlockSpec((None, N_HEADS, HEAD_DIM, PAGE_SIZE), per_b4),
            pl.BlockSpec((None, N_HEADS, PAGE_SIZE), lambda b, j, pt: (b, 0, 0)),
            pl.BlockSpec((None, g, N_HEADS, PAGE_SIZE), lambda b, j, pt: (b, j, 0, 0)),
        ] + [page_spec(i) for i in range(g)] + [page_spec(i) for i in range(g)],
        out_specs=pl.BlockSpec((None, N_HEADS, DEC_SEQ, HEAD_DIM), per_b4),
        scratch_shapes=[state, state, pltpu.VMEM((N_HEADS, DEC_SEQ, HEAD_DIM), f32), state],
    )
    return pl.pallas_call(
        _fox_sample_kernel,
        grid_spec=grid_spec,
        out_shape=jax.ShapeDtypeStruct((DEC_BATCH, N_HEADS, DEC_SEQ, HEAD_DIM), f32),
        compiler_params=pltpu.CompilerParams(
            dimension_semantics=("parallel", "arbitrary"), vmem_limit_bytes=VMEM_LIMIT),
        name="fox_sample",
    )(page_table, q4, kn_t, vn_t, lfn_t, sfx, *([cache_kt] * g), *([cache_vt] * g))


def _oproj_kernel(x_ref, a_ref, m_ref, wa_ref, wm_ref, o_ref):
    o_ref[...] = x_ref[...] + _dot(a_ref[...], wa_ref[...]) + _dot(m_ref[...], wm_ref[...])


def _out_proj(x, att, gm, wa, wm):
    row = lambda i: (i, 0)
    fixed = lambda i: (0, 0)
    return pl.pallas_call(
        _oproj_kernel,
        grid=(N_ROWS // PROJ_TM,),
        in_specs=[
            pl.BlockSpec((PROJ_TM, D_MODEL), row),
            pl.BlockSpec((PROJ_TM, ATTN_WIDTH), row),
            pl.BlockSpec((PROJ_TM, GMLP_WIDTH), row),
            pl.BlockSpec((ATTN_WIDTH, D_MODEL), fixed),
            pl.BlockSpec((GMLP_WIDTH, D_MODEL), fixed),
        ],
        out_specs=pl.BlockSpec((PROJ_TM, D_MODEL), row),
        out_shape=jax.ShapeDtypeStruct((N_ROWS, D_MODEL), f32),
        compiler_params=pltpu.CompilerParams(
            dimension_semantics=("parallel",), vmem_limit_bytes=VMEM_LIMIT),
        name="out_proj",
    )(x, att, gm, wa, wm)


def _pad_ff(w, axis):
    pad = [(0, 0), (0, 0)]
    pad[axis] = (0, D_FF_PAD - D_FF)
    return jnp.pad(w.astype(bf16), pad)


def _keys_last(x):
    xt = x.transpose(0, 2, 3, 1)
    return jnp.pad(xt, ((0, 0), (0, 0), (0, 0), (0, PAGE_SIZE - DEC_SEQ)))


def kernel(x_prompt, x_sample, cache_k, cache_v, cache_logf, page_table, g_ffn1, w1_gate, w1_up, w1_down, g_mix, w_in, b_f, g_gv, w_s, b_s, w_out, g_ffn2, w2_gate, w2_up, w2_down, g_final):
    l = 0
    x = jnp.concatenate([x_prompt.reshape(N_PROMPT, D_MODEL), x_sample.reshape(N_SAMPLE, D_MODEL)], axis=0)
    g_fin = g_final.reshape(1, D_MODEL)

    w1g, w1u, w1d = _pad_ff(w1_gate[l], 1), _pad_ff(w1_up[l], 1), _pad_ff(w1_down[l], 0)
    w2g, w2u, w2d = _pad_ff(w2_gate[l], 1), _pad_ff(w2_up[l], 1), _pad_ff(w2_down[l], 0)
    win = w_in[l]
    a = ATTN_WIDTH
    wq = win[:, 0:a].astype(bf16)
    wk = win[:, a:2 * a].astype(bf16)
    wv = win[:, 2 * a:3 * a].astype(bf16)
    wvt = wv.T
    wf = jnp.pad(win[:, 3 * a:3 * a + N_HEADS], ((0, 0), (0, LANES - N_HEADS))).astype(bf16)
    wu = win[:, 3 * a + N_HEADS:3 * a + N_HEADS + GMLP_WIDTH].astype(bf16)
    wgv = win[:, 3 * a + N_HEADS + GMLP_WIDTH:].astype(bf16)
    bfp = jnp.pad(b_f[l].reshape(1, N_HEADS), ((0, 0), (0, LANES - N_HEADS)))
    wo = w_out[l].astype(bf16)

    reps = CHUNK // DEC_SEQ
    ws2 = jnp.stack([w_s[l], jnp.tile(w_s[l][:, :DEC_SEQ, :DEC_SEQ], (1, reps, reps))])
    ti = jnp.arange(CHUNK)
    tril = (ti[None, :] <= ti[:, None])
    same = (ti[None, :] // DEC_SEQ) == (ti[:, None] // DEC_SEQ)
    mask2 = jnp.stack([tril, tril & same]).astype(f32)
    bs2 = jnp.stack([b_s[l].T, jnp.tile(b_s[l][:, :DEC_SEQ], (1, reps)).T])

    x1 = _ffn(x, g_ffn1[l].reshape(1, D_MODEL), w1g, w1u, w1d, g_fin, False)

    gm = g_mix[l].reshape(1, D_MODEL)
    q32, qb, k32, kb, v32, vt, lf_pad = _attn_proj(x1, gm, wq, wk, wv, wvt, wf, bfp)
    gmlp_out, gvn_s = _gmlp(x1, gm, wu, wgv, g_gv[l], ws2, mask2, bs2)

    c = _prompt_cumsum(lf_pad)
    att_p = _fox_prompt(qb, kb, vt, c).T

    shp_s = (DEC_BATCH, DEC_SEQ, N_HEADS, HEAD_DIM)
    k_s = k32[N_PROMPT:].reshape(shp_s)
    v_s = v32[N_PROMPT:].reshape(shp_s)
    q4 = q32[N_PROMPT:].reshape(shp_s).transpose(0, 2, 1, 3)
    lf = lf_pad[:, :N_HEADS]
    lfn_t = jnp.pad(lf[N_PROMPT:].reshape(DEC_BATCH, DEC_SEQ, N_HEADS).transpose(0, 2, 1),
                    ((0, 0), (0, 0), (0, PAGE_SIZE - DEC_SEQ)))
    cache_kt = cache_k[l].transpose(0, 2, 3, 1)
    cache_vt = cache_v[l].transpose(0, 2, 3, 1)
    cache_lft = cache_logf[l].transpose(0, 2, 1)
    sfx = _cache_suffix(page_table, cache_lft)
    att_s = _fox_sample(page_table, q4, _keys_last(k_s), _keys_last(v_s), lfn_t, sfx, cache_kt, cache_vt)
    att_s = att_s.transpose(0, 2, 1, 3).reshape(N_SAMPLE, ATTN_WIDTH).astype(bf16)

    att = jnp.concatenate([att_p, att_s], axis=0)
    x2 = _out_proj(x1, att, gmlp_out, wo[:ATTN_WIDTH], wo[ATTN_WIDTH:])
    y = _ffn(x2, g_ffn2[l].reshape(1, D_MODEL), w2g, w2u, w2d, g_fin, True)

    shp = (1, BATCH, SEQ, N_HEADS, HEAD_DIM)
    return (y[:N_PROMPT].reshape(BATCH, SEQ, D_MODEL),
            y[N_PROMPT:].reshape(DEC_BATCH, DEC_SEQ, D_MODEL),
            k32[:N_PROMPT].reshape(shp),
            v32[:N_PROMPT].reshape(shp),
            lf[:N_PROMPT].reshape(1, BATCH, SEQ, N_HEADS),
            k_s[None], v_s[None],
            lf[N_PROMPT:].reshape(1, DEC_BATCH, DEC_SEQ, N_HEADS),
            gvn_s.reshape(1, DEC_BATCH, DEC_SEQ, N_GMLP_HEADS, GMLP_HEAD_DIM))
```

```python
import functools

import jax
import jax.numpy as jnp
from jax import lax
from jax.experimental import pallas as pl
from jax.experimental.pallas import tpu as pltpu

f32 = jnp.float32
bf16 = jnp.bfloat16

D_MODEL = 2048
BATCH = 2
SEQ = 4096
DEC_BATCH = 32
DEC_SEQ = 8
PAST_LEN = 16384
PAGE_SIZE = 128
ATTN_WIDTH = 1024
GMLP_WIDTH = 1024
HEAD_DIM = 64
N_HEADS = 16
CHUNK = 128
GMLP_HEAD_DIM = 128
N_GMLP_HEADS = 8
D_FF = 5504
EPS = 1e-6
NEG_INF = -1e30
N_PAGES = PAST_LEN // PAGE_SIZE

LANES = 128
SUBLANES = 8
MXU_DIM = 256

N_PROMPT = BATCH * SEQ
N_SAMPLE = DEC_BATCH * DEC_SEQ
N_ROWS = N_PROMPT + N_SAMPLE

FFN_TM = 768
FFN_TF = 512
FFN_NF = -(-D_FF // FFN_TF)
FFN_TAIL = D_FF - (FFN_NF - 1) * FFN_TF
FFN_PROMPT_TAIL = N_PROMPT % FFN_TM
assert FFN_PROMPT_TAIL + N_SAMPLE == FFN_TM and FFN_PROMPT_TAIL % SUBLANES == 0
PROJ_TM = 256
ATT_T = 512
PAGES_PER_STEP = 16
SFX_PAGES_PER_STEP = 64
VMEM_LIMIT = 56 * 1024 * 1024


def _rms(x, g):
    return x * lax.rsqrt(jnp.mean(x * x, axis=-1, keepdims=True) + EPS) * g


def _split3(x):
    hi = x.astype(bf16)
    r1 = x - hi.astype(f32)
    mid = r1.astype(bf16)
    lo = (r1 - mid.astype(f32)).astype(bf16)
    return hi, mid, lo


def _dot(a, b):
    return jnp.dot(a, b, preferred_element_type=f32)


def _dot_nt(a, b):
    return lax.dot_general(a, b, (((1,), (1,)), ((), ())), preferred_element_type=f32)


def _dot3(x, sel):
    hi, mid, lo = _split3(x)
    return _dot(hi, sel) + _dot(mid, sel) + _dot(lo, sel)


def _ffn_kernel(*refs, first, nf):
    if first:
        x_ref, xs_ref, g_ref, wg_ref, wu_ref, wd_ref, gfin_ref, o_ref, h_ref = refs
    else:
        x_ref, g_ref, wg_ref, wu_ref, wd_ref, gfin_ref, o_ref, os_ref, h_ref = refs
    i = pl.program_id(0)
    f = pl.program_id(1)
    last_tile = pl.num_programs(0) - 1
    final_norm = not first

    def start(x):
        h_ref[...] = _rms(x, g_ref[...]).astype(bf16)
        o_ref[...] = x

    if first:
        @pl.when((f == 0) & (i < last_tile))
        def _():
            start(x_ref[...])

        @pl.when((f == 0) & (i == last_tile))
        def _():
            start(jnp.concatenate([x_ref[0:FFN_PROMPT_TAIL, :], xs_ref[...]], axis=0))
    else:
        @pl.when(f == 0)
        def _():
            start(x_ref[...])

    def step(tail):
        h = h_ref[...]
        gate = _dot(h, wg_ref[...])
        up = _dot(h, wu_ref[...])
        act = (gate / (1.0 + jnp.exp(-gate))) * up * 0.5
        wd = wd_ref[...]
        if tail:
            col = lax.broadcasted_iota(jnp.int32, act.shape, 1)
            act = jnp.where(col < FFN_TAIL, act, 0.0)
            row = lax.broadcasted_iota(jnp.int32, wd.shape, 0)
            wd = jnp.where(row < FFN_TAIL, wd.astype(f32), 0.0).astype(bf16)
        o_ref[...] += _dot(act.astype(bf16), wd)

    @pl.when(f < nf - 1)
    def _():
        step(False)

    @pl.when(f == nf - 1)
    def _():
        step(True)
        if final_norm:
            o_ref[...] = _rms(o_ref[...], gfin_ref[...])

            @pl.when(i == last_tile)
            def _():
                os_ref[...] = o_ref[FFN_PROMPT_TAIL:, :]


def _ffn(xs, g, wg, wu, wd, gfin, first):
    nf = FFN_NF
    tile = pl.BlockSpec((FFN_TM, D_MODEL), lambda i, f: (i, 0))
    sample = pl.BlockSpec((N_SAMPLE, D_MODEL), lambda i, f: (0, 0))
    vec = pl.BlockSpec((1, D_MODEL), lambda i, f: (0, 0))
    weights = [pl.BlockSpec((D_MODEL, FFN_TF), lambda i, f: (0, f)),
               pl.BlockSpec((D_MODEL, FFN_TF), lambda i, f: (0, f)),
               pl.BlockSpec((FFN_TF, D_MODEL), lambda i, f: (f, 0))]
    if first:
        in_specs = [tile, sample, vec] + weights + [vec]
        out_specs = tile
        out_shape = jax.ShapeDtypeStruct((N_ROWS, D_MODEL), f32)
    else:
        in_specs = [tile, vec] + weights + [vec]
        out_specs = [tile, sample]
        out_shape = [jax.ShapeDtypeStruct((N_PROMPT, D_MODEL), f32),
                     jax.ShapeDtypeStruct((N_SAMPLE, D_MODEL), f32)]
    return pl.pallas_call(
        functools.partial(_ffn_kernel, first=first, nf=nf),
        grid=(N_ROWS // FFN_TM, nf),
        in_specs=in_specs,
        out_specs=out_specs,
        out_shape=out_shape,
        scratch_shapes=[pltpu.VMEM((FFN_TM, D_MODEL), bf16)],
        compiler_params=pltpu.CompilerParams(
            dimension_semantics=("arbitrary", "arbitrary"), vmem_limit_bytes=VMEM_LIMIT),
        name="ffn_half",
    )(*xs, g, wg, wu, wd, gfin)


N_PROMPT_TILES = N_PROMPT // PROJ_TM
TILES_PER_SEQ = SEQ // PROJ_TM
assert N_SAMPLE == PROJ_TM


def _aproj_kernel(x_ref, g_ref, wq_ref, wk_ref, wv_ref, wf_ref, bf_ref,
                  qb_ref, kb_ref, kt_ref, vt_ref, vtb_ref, lf_ref, qs_ref, kts_ref, vts_ref):
    i = pl.program_id(0)
    h = _rms(x_ref[...], g_ref[...]).astype(bf16)
    q = _dot_nt(h, wq_ref[...]) * (HEAD_DIM ** -0.5)
    qb_ref[...] = q.astype(bf16)
    kb_ref[...] = _dot_nt(h, wk_ref[...]).astype(bf16)
    kt = _dot_nt(wk_ref[...], h)
    vt = _dot_nt(wv_ref[...], h)
    z = _dot_nt(h, wf_ref[...]) + bf_ref[...]
    lf_ref[...] = -(jnp.maximum(-z, 0.0) + jnp.log1p(jnp.exp(-jnp.abs(z))))

    @pl.when(i < N_PROMPT_TILES)
    def _():
        kt_ref[...] = kt
        vt_ref[...] = vt
        vtb_ref[...] = vt.astype(bf16)

    @pl.when(i == N_PROMPT_TILES)
    def _():
        qs_ref[...] = q
        kts_ref[...] = kt
        vts_ref[...] = vt


def _attn_proj(x, g, wt, wf, bfp):
    row = lambda i: (i, 0)
    fixed = lambda i: (0, 0)

    def seq_tile(i):
        t = jnp.minimum(i, N_PROMPT_TILES - 1)
        return (t // TILES_PER_SEQ, 0, t % TILES_PER_SEQ)

    wide = lambda dt: jax.ShapeDtypeStruct((N_ROWS, ATTN_WIDTH), dt)
    tposed = lambda dt: jax.ShapeDtypeStruct((BATCH, ATTN_WIDTH, SEQ), dt)
    small_t = jax.ShapeDtypeStruct((ATTN_WIDTH, N_SAMPLE), f32)
    w_block = lambda n: pl.BlockSpec((ATTN_WIDTH, D_MODEL), lambda i: (n, 0))
    return pl.pallas_call(
        _aproj_kernel,
        grid=(N_ROWS // PROJ_TM,),
        in_specs=[
            pl.BlockSpec((PROJ_TM, D_MODEL), row),
            pl.BlockSpec((1, D_MODEL), fixed),
            w_block(0), w_block(1), w_block(2),
            pl.BlockSpec((LANES, D_MODEL), fixed),
            pl.BlockSpec((1, LANES), fixed),
        ],
        out_specs=[pl.BlockSpec((PROJ_TM, ATTN_WIDTH), row)] * 2
        + [pl.BlockSpec((None, ATTN_WIDTH, PROJ_TM), seq_tile)] * 3
        + [pl.BlockSpec((PROJ_TM, LANES), row),
           pl.BlockSpec((N_SAMPLE, ATTN_WIDTH), fixed),
           pl.BlockSpec((ATTN_WIDTH, N_SAMPLE), fixed),
           pl.BlockSpec((ATTN_WIDTH, N_SAMPLE), fixed)],
        out_shape=[wide(bf16), wide(bf16), tposed(f32), tposed(f32), tposed(bf16),
                   jax.ShapeDtypeStruct((N_ROWS, LANES), f32),
                   jax.ShapeDtypeStruct((N_SAMPLE, ATTN_WIDTH), f32), small_t, small_t],
        compiler_params=pltpu.CompilerParams(
            dimension_semantics=("arbitrary",), vmem_limit_bytes=VMEM_LIMIT),
        name="attn_proj",
    )(x, g, wt, wt, wt, wf, bfp)


def _gelu_tanh(x):
    return 0.5 * x * (1.0 + jnp.tanh(0.7978845608028654 * (x + 0.044715 * (x * x * x))))


def _gmlp_kernel(x_ref, g_ref, wu_ref, wv_ref, ggv_ref, ws_ref, mask_ref, bs_ref,
                 o_ref, gvn_ref, *, n_tiles):
    i = pl.program_id(0)
    h = _rms(x_ref[...], g_ref[...]).astype(bf16)
    u = _gelu_tanh(_dot_nt(h, wu_ref[...]))
    gv = _gelu_tanh(_dot_nt(h, wv_ref[...]))
    mask = mask_ref[...]
    bias = bs_ref[...]
    for g in range(N_GMLP_HEADS):
        cols = slice(g * GMLP_HEAD_DIM, (g + 1) * GMLP_HEAD_DIM)
        gvn = _rms(gv[:, cols], ggv_ref[g:g + 1, :])
        w = (ws_ref[g] * mask).astype(bf16)
        for c in range(PROJ_TM // CHUNK):
            rows = slice(c * CHUNK, (c + 1) * CHUNK)
            mixed = _dot(w, gvn[rows].astype(bf16)) + bias[:, g:g + 1]
            o_ref[rows, cols] = (u[rows, cols] * mixed).astype(bf16)

        @pl.when(i == n_tiles - 1)
        def _():
            gvn_ref[:, cols] = gvn


def _gmlp(x, g, wu, wv, ggv, ws2, mask2, bs2):
    n_tiles = N_ROWS // PROJ_TM
    n_prompt_tiles = N_PROMPT // PROJ_TM
    row = lambda i: (i, 0)
    fixed = lambda i: (0, 0)
    kind3 = lambda i: (i // n_prompt_tiles, 0, 0)
    kind4 = lambda i: (i // n_prompt_tiles, 0, 0, 0)
    return pl.pallas_call(
        functools.partial(_gmlp_kernel, n_tiles=n_tiles),
        grid=(n_tiles,),
        in_specs=[
            pl.BlockSpec((PROJ_TM, D_MODEL), row),
            pl.BlockSpec((1, D_MODEL), fixed),
            pl.BlockSpec((GMLP_WIDTH, D_MODEL), fixed),
            pl.BlockSpec((GMLP_WIDTH, D_MODEL), fixed),
            pl.BlockSpec((N_GMLP_HEADS, GMLP_HEAD_DIM), fixed),
            pl.BlockSpec((None, N_GMLP_HEADS, CHUNK, CHUNK), kind4),
            pl.BlockSpec((None, CHUNK, CHUNK), kind3),
            pl.BlockSpec((None, CHUNK, N_GMLP_HEADS), kind3),
        ],
        out_specs=[pl.BlockSpec((PROJ_TM, GMLP_WIDTH), row),
                   pl.BlockSpec((PROJ_TM, GMLP_WIDTH), fixed)],
        out_shape=[jax.ShapeDtypeStruct((N_ROWS, GMLP_WIDTH), bf16),
                   jax.ShapeDtypeStruct((N_SAMPLE, GMLP_WIDTH), f32)],
        compiler_params=pltpu.CompilerParams(
            dimension_semantics=("arbitrary",), vmem_limit_bytes=VMEM_LIMIT),
        name="gmlp_group",
    )(x, g, wu, wv, ggv, ws2, mask2, bs2)


def _cumsum_kernel(lf_ref, c_ref):
    r = lax.broadcasted_iota(jnp.int32, (CHUNK, CHUNK), 0)
    s = lax.broadcasted_iota(jnp.int32, (CHUNK, CHUNK), 1)
    tri = (s <= r).astype(bf16)

    def body(j, carry):
        rows = pl.ds(pl.multiple_of(j * CHUNK, CHUNK), CHUNK)
        hi, mid, lo = _split3(lf_ref[rows, :])
        c = _dot(tri, hi) + _dot(tri, mid) + _dot(tri, lo) + carry
        c_ref[rows, :] = c
        return c[CHUNK - 1:CHUNK, :]

    lax.fori_loop(0, SEQ // CHUNK, body, jnp.zeros((1, LANES), f32))


def _prompt_cumsum(lf_pad):
    return pl.pallas_call(
        _cumsum_kernel,
        grid=(BATCH,),
        in_specs=[pl.BlockSpec((SEQ, LANES), lambda b: (b, 0))],
        out_specs=pl.BlockSpec((SEQ, LANES), lambda b: (b, 0)),
        out_shape=jax.ShapeDtypeStruct((N_PROMPT, LANES), f32),
        compiler_params=pltpu.CompilerParams(
            dimension_semantics=("parallel",), vmem_limit_bytes=VMEM_LIMIT),
        name="logf_cumsum",
    )(lf_pad)


N_SPLIT = 3


def _bias_lanes(e):
    return HEAD_DIM * (1 - e)


def _fox_prompt_kernel(q_ref, k_ref, vt_ref, c_ref, o_ref,
                       kp_ref, g_ref, qp_ref, sa_ref, sb_ref, m_ref, l_ref, acc_ref):
    hp = pl.program_id(1)
    qi = pl.program_id(2)
    t = ATT_T
    lane = lax.broadcasted_iota(jnp.int32, (t, LANES), 1)

    def in_span(e, first, n):
        base = _bias_lanes(e) + first
        return (lane >= base) & (lane < base + n)

    @pl.when(qi == 0)
    def _():
        r = lax.broadcasted_iota(jnp.int32, (LANES, LANES), 0)
        l2 = lax.broadcasted_iota(jnp.int32, (LANES, LANES), 1)
        sels = []
        for j in range(N_SPLIT):
            sel = jnp.zeros((LANES, LANES), f32)
            for e in range(2):
                src = r == 2 * hp + e
                sel = sel - (src & (l2 == _bias_lanes(e) + j)).astype(f32)
                sel = sel + (src & (l2 == _bias_lanes(e) + N_SPLIT + j)).astype(f32)
            sels.append(sel.astype(bf16))

        def prep(i, _):
            rows = pl.ds(pl.multiple_of(i * t, t), t)
            parts = _split3(c_ref[rows, :])
            g = sum(_dot(p, s) for p, s in zip(parts, sels))
            g_ref[rows, :] = g
            k2 = k_ref[rows, :].astype(f32)
            for e in range(2):
                own = (lane < HEAD_DIM) if e == 0 else (lane >= HEAD_DIM)
                extra = jnp.where(in_span(e, 0, N_SPLIT), g,
                                  jnp.where(in_span(e, N_SPLIT, N_SPLIT), 1.0, 0.0))
                kp_ref[e, rows, :] = jnp.where(own, k2, extra).astype(bf16)
            return 0

        lax.fori_loop(0, SEQ // t, prep, 0)

    q0 = pl.multiple_of(qi * t, t)
    q2 = q_ref[...].astype(f32)
    gq = g_ref[pl.ds(q0, t), :]
    for e in range(2):
        own = (lane < HEAD_DIM) if e == 0 else (lane >= HEAD_DIM)
        extra = jnp.where(in_span(e, 0, N_SPLIT), 1.0,
                          jnp.where(in_span(e, N_SPLIT, N_SPLIT), gq, 0.0))
        qp_ref[e] = jnp.where(own, q2, extra).astype(bf16)

    m_ref[...] = jnp.full(m_ref.shape, NEG_INF, f32)
    l_ref[...] = jnp.zeros(l_ref.shape, f32)
    acc_ref[...] = jnp.zeros(acc_ref.shape, f32)

    def scores(blk, s_ref):
        k0 = pl.multiple_of(blk * t, t)
        for e in range(2):
            s_ref[e] = _dot_nt(kp_ref[e, pl.ds(k0, t), :], qp_ref[e])

    def consume(blk, s_ref, on_diagonal):
        k0 = pl.multiple_of(blk * t, t)
        for e in range(2):
            st = s_ref[e]
            if on_diagonal:
                key = lax.broadcasted_iota(jnp.int32, (t, t), 0)
                qry = lax.broadcasted_iota(jnp.int32, (t, t), 1)
                st = jnp.where(key <= qry, st, NEG_INF)
            m_old = m_ref[e]
            m_new = jnp.maximum(m_old, jnp.max(st, axis=0, keepdims=True))
            alpha = jnp.exp(m_old - m_new)
            pt = jnp.exp(st - m_new)
            l_ref[e] = alpha * l_ref[e] + jnp.sum(pt, axis=0, keepdims=True)
            m_ref[e] = m_new
            vt = vt_ref[e * HEAD_DIM:(e + 1) * HEAD_DIM, pl.ds(k0, t)]
            acc_ref[e] = acc_ref[e] * alpha + _dot(vt, pt.astype(bf16))

    scores(0, sa_ref)

    def pair(p, _):
        j = 2 * p
        scores(j + 1, sb_ref)
        consume(j, sa_ref, False)
        scores(j + 2, sa_ref)
        consume(j + 1, sb_ref, False)
        return 0

    lax.fori_loop(0, qi // 2, pair, 0)

    @pl.when(qi % 2 == 0)
    def _():
        consume(qi, sa_ref, True)

    @pl.when(qi % 2 == 1)
    def _():
        scores(qi, sb_ref)
        consume(qi - 1, sa_ref, False)
        consume(qi, sb_ref, True)

    for e in range(2):
        o_ref[e * HEAD_DIM:(e + 1) * HEAD_DIM, :] = (acc_ref[e] / l_ref[e]).astype(bf16)


def _fox_prompt(qb, kb, vt, c):
    nq = SEQ // ATT_T
    return pl.pallas_call(
        _fox_prompt_kernel,
        grid=(BATCH, N_HEADS // 2, nq),
        in_specs=[
            pl.BlockSpec((ATT_T, LANES), lambda b, hp, qi: (b * nq + qi, hp)),
            pl.BlockSpec((SEQ, LANES), lambda b, hp, qi: (b, hp)),
            pl.BlockSpec((None, LANES, SEQ), lambda b, hp, qi: (b, hp, 0)),
            pl.BlockSpec((SEQ, LANES), lambda b, hp, qi: (b, 0)),
        ],
        out_specs=pl.BlockSpec((LANES, ATT_T), lambda b, hp, qi: (hp, b * nq + qi)),
        out_shape=jax.ShapeDtypeStruct((ATTN_WIDTH, N_PROMPT), bf16),
        scratch_shapes=[pltpu.VMEM((2, SEQ, LANES), bf16), pltpu.VMEM((SEQ, LANES), f32),
                        pltpu.VMEM((2, ATT_T, LANES), bf16),
                        pltpu.VMEM((2, ATT_T, ATT_T), f32), pltpu.VMEM((2, ATT_T, ATT_T), f32),
                        pltpu.VMEM((2, 1, ATT_T), f32), pltpu.VMEM((2, 1, ATT_T), f32),
                        pltpu.VMEM((2, HEAD_DIM, ATT_T), f32)],
        compiler_params=pltpu.CompilerParams(
            dimension_semantics=("parallel", "parallel", "arbitrary"),
            vmem_limit_bytes=VMEM_LIMIT),
        name="fox_prompt",
    )(qb, kb, vt, c)


def _suffix_kernel(pt_ref, *refs):
    g = SFX_PAGES_PER_STEP
    lf_refs = refs[:g]
    out_ref, carry_ref = refs[g:]
    j = pl.program_id(1)

    @pl.when(j == 0)
    def _():
        carry_ref[...] = jnp.zeros(carry_ref.shape, f32)

    j_i = lax.broadcasted_iota(jnp.int32, (PAGE_SIZE, PAGE_SIZE), 0)
    s_i = lax.broadcasted_iota(jnp.int32, (PAGE_SIZE, PAGE_SIZE), 1)
    after = (j_i > s_i).astype(bf16)

    x = jnp.concatenate([lf_refs[i][...] for i in range(g)], axis=0)
    within = _dot3(x, after)
    for i in range(g):
        rows = slice(i * N_HEADS, (i + 1) * N_HEADS)
        carry = carry_ref[...]
        out_ref[g - 1 - i] = within[rows] + carry
        carry_ref[...] = carry + jnp.sum(x[rows], axis=1, keepdims=True)


def _cache_suffix(page_table, cache_lft):
    g = SFX_PAGES_PER_STEP
    nj = N_PAGES // g

    def lf_map(i):
        return lambda b, j, pt: (pt[b, N_PAGES - 1 - (j * g + i)], 0, 0)

    grid_spec = pltpu.PrefetchScalarGridSpec(
        num_scalar_prefetch=1,
        grid=(DEC_BATCH, nj),
        in_specs=[pl.BlockSpec((None, N_HEADS, PAGE_SIZE), lf_map(i)) for i in range(g)],
        out_specs=pl.BlockSpec((None, g, N_HEADS, PAGE_SIZE), lambda b, j, pt: (b, nj - 1 - j, 0, 0)),
        scratch_shapes=[pltpu.VMEM((N_HEADS, 1), f32)],
    )
    return pl.pallas_call(
        _suffix_kernel,
        grid_spec=grid_spec,
        out_shape=jax.ShapeDtypeStruct((DEC_BATCH, N_PAGES, N_HEADS, PAGE_SIZE), f32),
        compiler_params=pltpu.CompilerParams(
            dimension_semantics=("parallel", "arbitrary"), vmem_limit_bytes=VMEM_LIMIT),
        name="cache_suffix",
    )(page_table, *([cache_lft] * g))


def _fox_sample_kernel(pt_ref, q_ref, kn_ref, vn_ref, lfn_ref, sfx_ref, *refs):
    g = PAGES_PER_STEP
    k_refs = refs[:g]
    v_refs = refs[g:2 * g]
    o_ref, m_ref, l_ref, acc_ref, cn_ref = refs[2 * g:]
    j = pl.program_id(1)
    tok = lax.broadcasted_iota(jnp.int32, (DEC_SEQ, PAGE_SIZE), 0)
    key = lax.broadcasted_iota(jnp.int32, (DEC_SEQ, PAGE_SIZE), 1)

    @pl.when(j == 0)
    def _():
        j_i = lax.broadcasted_iota(jnp.int32, (PAGE_SIZE, PAGE_SIZE), 0)
        s_i = lax.broadcasted_iota(jnp.int32, (PAGE_SIZE, PAGE_SIZE), 1)
        upto = (j_i <= s_i).astype(bf16)
        c_new = _dot3(lfn_ref[...], upto)

        for h in range(N_HEADS):
            c_row = c_new[h:h + 1, :]
            c_col = jnp.sum(jnp.where(key == tok, c_row, 0.0), axis=1, keepdims=True)
            c_b = jnp.broadcast_to(c_col, (DEC_SEQ, PAGE_SIZE))
            cn_ref[h] = c_b
            s = _dot(q_ref[h], kn_ref[h]) + (c_b - c_row)
            s = jnp.where(key <= tok, s, NEG_INF)
            m = jnp.max(s, axis=1, keepdims=True)
            p = jnp.exp(s - m)
            m_ref[h] = jnp.broadcast_to(m, (DEC_SEQ, PAGE_SIZE))
            l_ref[h] = p
            acc_ref[h] = _dot_nt(p, vn_ref[h])

    scores, row_max = [], []
    for h in range(N_HEADS):
        q = q_ref[h]
        bias = cn_ref[h]
        s = [_dot(q, k_refs[pg][h]) + (bias + sfx_ref[pg, h:h + 1, :]) for pg in range(g)]
        mx = s[0]
        for pg in range(1, g):
            mx = jnp.maximum(mx, s[pg])
        scores.append(s)
        row_max.append(jnp.max(mx, axis=1, keepdims=True))
    for h in range(N_HEADS):
        s = scores[h]
        m_old = m_ref[h]
        m_new = jnp.maximum(m_old, row_max[h])
        alpha = jnp.exp(m_old - m_new)
        p = [jnp.exp(s[pg] - m_new) for pg in range(g)]
        l_ref[h] = alpha * l_ref[h] + sum(p)
        m_ref[h] = m_new
        pv = sum(_dot_nt(p[pg], v_refs[pg][h]) for pg in range(g))
        acc_ref[h] = acc_ref[h] * alpha[:, :HEAD_DIM] + pv

    @pl.when(j == pl.num_programs(1) - 1)
    def _():
        for h in range(N_HEADS):
            o_ref[h] = acc_ref[h] / jnp.sum(l_ref[h], axis=1, keepdims=True)


def _fox_sample(page_table, q4, kn_t, vn_t, lfn_t, sfx, cache_kt, cache_vt):
    g = PAGES_PER_STEP
    nj = N_PAGES // g

    def page_map(i):
        return lambda b, j, pt: (pt[b, j * g + i], 0, 0, 0)

    per_b4 = lambda b, j, pt: (b, 0, 0, 0)
    page_spec = lambda i: pl.BlockSpec((None, N_HEADS, HEAD_DIM, PAGE_SIZE), page_map(i))
    state = pltpu.VMEM((N_HEADS, DEC_SEQ, PAGE_SIZE), f32)
    grid_spec = pltpu.PrefetchScalarGridSpec(
        num_scalar_prefetch=1,
        grid=(DEC_BATCH, nj),
        in_specs=[
            pl.BlockSpec((None, N_HEADS, DEC_SEQ, HEAD_DIM), per_b4),
            pl.BlockSpec((None, N_HEADS, HEAD_DIM, PAGE_SIZE), per_b4),
            pl.BlockSpec((None, N_HEADS, HEAD_DIM, PAGE_SIZE), per_b4),
            pl.BlockSpec((None, N_HEADS, PAGE_SIZE), lambda b, j, pt: (b, 0, 0)),
            pl.BlockSpec((None, g, N_HEADS, PAGE_SIZE), lambda b, j, pt: (b, j, 0, 0)),
        ] + [page_spec(i) for i in range(g)] + [page_spec(i) for i in range(g)],
        out_specs=pl.BlockSpec((None, N_HEADS, DEC_SEQ, HEAD_DIM), per_b4),
        scratch_shapes=[state, state, pltpu.VMEM((N_HEADS, DEC_SEQ, HEAD_DIM), f32), state],
    )
    return pl.pallas_call(
        _fox_sample_kernel,
        grid_spec=grid_spec,
        out_shape=jax.ShapeDtypeStruct((DEC_BATCH, N_HEADS, DEC_SEQ, HEAD_DIM), f32),
        compiler_params=pltpu.CompilerParams(
            dimension_semantics=("parallel", "arbitrary"), vmem_limit_bytes=VMEM_LIMIT),
        name="fox_sample",
    )(page_table, q4, kn_t, vn_t, lfn_t, sfx, *([cache_kt] * g), *([cache_vt] * g))


def _oproj_kernel(x_ref, at_ref, as_ref, m_ref, wo_ref, o_ref):
    i = pl.program_id(0)
    wa = wo_ref[0:ATTN_WIDTH, :]
    base = x_ref[...] + _dot(m_ref[...], wo_ref[ATTN_WIDTH:, :])

    @pl.when(i < N_PROMPT_TILES)
    def _():
        o_ref[...] = base + lax.dot_general(at_ref[...], wa, (((0,), (0,)), ((), ())),
                                            preferred_element_type=f32)

    @pl.when(i == N_PROMPT_TILES)
    def _():
        o_ref[...] = base + _dot(as_ref[...], wa)


def _out_proj(x, att_t, att_s, gm, wo):
    row = lambda i: (i, 0)
    fixed = lambda i: (0, 0)
    return pl.pallas_call(
        _oproj_kernel,
        grid=(N_ROWS // PROJ_TM,),
        in_specs=[
            pl.BlockSpec((PROJ_TM, D_MODEL), row),
            pl.BlockSpec((ATTN_WIDTH, PROJ_TM), lambda i: (0, jnp.minimum(i, N_PROMPT_TILES - 1))),
            pl.BlockSpec((N_SAMPLE, ATTN_WIDTH), fixed),
            pl.BlockSpec((PROJ_TM, GMLP_WIDTH), row),
            pl.BlockSpec((ATTN_WIDTH + GMLP_WIDTH, D_MODEL), fixed),
        ],
        out_specs=pl.BlockSpec((PROJ_TM, D_MODEL), row),
        out_shape=jax.ShapeDtypeStruct((N_ROWS, D_MODEL), f32),
        compiler_params=pltpu.CompilerParams(
            dimension_semantics=("arbitrary",), vmem_limit_bytes=VMEM_LIMIT),
        name="out_proj",
    )(x, att_t, att_s, gm, wo)


def kernel(x_prompt, x_sample, cache_k, cache_v, cache_logf, page_table, g_ffn1, w1_gate, w1_up, w1_down, g_mix, w_in, b_f, g_gv, w_s, b_s, w_out, g_ffn2, w2_gate, w2_up, w2_down, g_final):
    l = 0
    x_in = (x_prompt.reshape(N_PROMPT, D_MODEL), x_sample.reshape(N_SAMPLE, D_MODEL))
    g_fin = g_final.reshape(1, D_MODEL)

    w1g, w1u, w1d = w1_gate[l].astype(bf16), w1_up[l].astype(bf16), w1_down[l].astype(bf16)
    w2g, w2u, w2d = w2_gate[l].astype(bf16), w2_up[l].astype(bf16), w2_down[l].astype(bf16)
    wt = w_in[l].T.astype(bf16)
    a = ATTN_WIDTH
    wf = jnp.pad(wt[3 * a:3 * a + N_HEADS], ((0, LANES - N_HEADS), (0, 0)))
    wu = wt[3 * a + N_HEADS:3 * a + N_HEADS + GMLP_WIDTH]
    wgv = wt[3 * a + N_HEADS + GMLP_WIDTH:]
    bfp = jnp.pad(b_f[l].reshape(1, N_HEADS), ((0, 0), (0, LANES - N_HEADS)))
    wo = w_out[l].astype(bf16)

    reps = CHUNK // DEC_SEQ
    ws2 = jnp.stack([w_s[l], jnp.tile(w_s[l][:, :DEC_SEQ, :DEC_SEQ], (1, reps, reps))])
    ti = jnp.arange(CHUNK)
    tril = (ti[None, :] <= ti[:, None])
    same = (ti[None, :] // DEC_SEQ) == (ti[:, None] // DEC_SEQ)
    mask2 = jnp.stack([tril, tril & same]).astype(f32)
    bs2 = jnp.stack([b_s[l].T, jnp.tile(b_s[l][:, :DEC_SEQ], (1, reps)).T])

    x1 = _ffn(x_in, g_ffn1[l].reshape(1, D_MODEL), w1g, w1u, w1d, g_fin, True)

    gm = g_mix[l].reshape(1, D_MODEL)
    qb, kb, kt_p, vt_p, vtb, lf_pad, q_s, kt_s, vt_s = _attn_proj(x1, gm, wt, wf, bfp)
    gmlp_out, gvn_s = _gmlp(x1, gm, wu, wgv, g_gv[l], ws2, mask2, bs2)

    c = _prompt_cumsum(lf_pad)
    att_t = _fox_prompt(qb, kb, vtb, c)

    def sample_keys_last(xt):
        return xt.reshape(N_HEADS, HEAD_DIM, DEC_BATCH, DEC_SEQ).transpose(2, 0, 1, 3)

    def as_page(x4):
        return jnp.pad(x4, ((0, 0), (0, 0), (0, 0), (0, PAGE_SIZE - DEC_SEQ)))

    k_s4, v_s4 = sample_keys_last(kt_s), sample_keys_last(vt_s)
    q4 = q_s.reshape(DEC_BATCH, DEC_SEQ, N_HEADS, HEAD_DIM).transpose(0, 2, 1, 3)
    lf = lf_pad[:, :N_HEADS]
    lfn_t = jnp.pad(lf[N_PROMPT:].reshape(DEC_BATCH, DEC_SEQ, N_HEADS).transpose(0, 2, 1),
                    ((0, 0), (0, 0), (0, PAGE_SIZE - DEC_SEQ)))
    cache_kt = cache_k[l].transpose(0, 2, 3, 1)
    cache_vt = cache_v[l].transpose(0, 2, 3, 1)
    cache_lft = cache_logf[l].transpose(0, 2, 1)
    sfx = _cache_suffix(page_table, cache_lft)
    att_s = _fox_sample(page_table, q4, as_page(k_s4), as_page(v_s4), lfn_t, sfx, cache_kt, cache_vt)
    att_s = att_s.transpose(0, 2, 1, 3).reshape(N_SAMPLE, ATTN_WIDTH).astype(bf16)

    x2 = _out_proj(x1, att_t, att_s, gmlp_out, wo)
    y_p, y_s = _ffn((x2,), g_ffn2[l].reshape(1, D_MODEL), w2g, w2u, w2d, g_fin, False)

    def prompt_rows(xt):
        return xt.reshape(BATCH, N_HEADS, HEAD_DIM, SEQ).transpose(0, 3, 1, 2)[None]

    def sample_rows(x4):
        return x4.transpose(0, 3, 1, 2)[None]

    return (y_p.reshape(BATCH, SEQ, D_MODEL),
            y_s.reshape(DEC_BATCH, DEC_SEQ, D_MODEL),
            prompt_rows(kt_p), prompt_rows(vt_p),
            lf[:N_PROMPT].reshape(1, BATCH, SEQ, N_HEADS),
            sample_rows(k_s4), sample_rows(v_s4),
            lf[N_PROMPT:].reshape(1, DEC_BATCH, DEC_SEQ, N_HEADS),
            gvn_s.reshape(1, DEC_BATCH, DEC_SEQ, N_GMLP_HEADS, GMLP_HEAD_DIM))
```

```python
import functools

import jax
import jax.numpy as jnp
from jax import lax
from jax.experimental import pallas as pl
from jax.experimental.pallas import tpu as pltpu

f32 = jnp.float32
bf16 = jnp.bfloat16

D_MODEL = 2048
BATCH = 2
SEQ = 4096
DEC_BATCH = 32
DEC_SEQ = 8
PAST_LEN = 16384
PAGE_SIZE = 128
ATTN_WIDTH = 1024
GMLP_WIDTH = 1024
HEAD_DIM = 64
N_HEADS = 16
CHUNK = 128
GMLP_HEAD_DIM = 128
N_GMLP_HEADS = 8
D_FF = 5504
EPS = 1e-6
NEG_INF = -1e30
N_PAGES = PAST_LEN // PAGE_SIZE

LANES = 128
SUBLANES = 8
MXU_DIM = 256

N_PROMPT = BATCH * SEQ
N_SAMPLE = DEC_BATCH * DEC_SEQ
N_ROWS = N_PROMPT + N_SAMPLE

FFN_TM = 768
FFN_TF = 512
FFN_NF = -(-D_FF // FFN_TF)
FFN_TAIL = D_FF - (FFN_NF - 1) * FFN_TF
FFN_PROMPT_TAIL = N_PROMPT % FFN_TM
assert FFN_PROMPT_TAIL + N_SAMPLE == FFN_TM and FFN_PROMPT_TAIL % SUBLANES == 0
PROJ_TM = 256
ATT_T = 512
PAGES_PER_STEP = 16
VMEM_LIMIT = 56 * 1024 * 1024


def _rms(x, g):
    return x * lax.rsqrt(jnp.mean(x * x, axis=-1, keepdims=True) + EPS) * g


def _split3(x):
    hi = x.astype(bf16)
    r1 = x - hi.astype(f32)
    mid = r1.astype(bf16)
    lo = (r1 - mid.astype(f32)).astype(bf16)
    return hi, mid, lo


def _dot(a, b):
    return jnp.dot(a, b, preferred_element_type=f32)


def _dot_nt(a, b):
    return lax.dot_general(a, b, (((1,), (1,)), ((), ())), preferred_element_type=f32)


def _dot3(x, sel):
    hi, mid, lo = _split3(x)
    return _dot(hi, sel) + _dot(mid, sel) + _dot(lo, sel)


def _ffn_kernel(*refs, first, nf):
    if first:
        x_ref, xs_ref, g_ref, wg_ref, wu_ref, wd_ref, gfin_ref, o_ref, h_ref = refs
    else:
        x_ref, g_ref, wg_ref, wu_ref, wd_ref, gfin_ref, o_ref, os_ref, h_ref = refs
    i = pl.program_id(0)
    f = pl.program_id(1)
    last_tile = pl.num_programs(0) - 1
    final_norm = not first

    def start(x):
        h_ref[...] = _rms(x, g_ref[...]).astype(bf16)
        o_ref[...] = x

    if first:
        @pl.when((f == 0) & (i < last_tile))
        def _():
            start(x_ref[...])

        @pl.when((f == 0) & (i == last_tile))
        def _():
            start(jnp.concatenate([x_ref[0:FFN_PROMPT_TAIL, :], xs_ref[...]], axis=0))
    else:
        @pl.when(f == 0)
        def _():
            start(x_ref[...])

    def step(tail):
        h = h_ref[...]
        gate = _dot(h, wg_ref[...])
        up = _dot(h, wu_ref[...])
        act = (gate / (1.0 + jnp.exp(-gate))) * up * 0.5
        wd = wd_ref[...]
        if tail:
            col = lax.broadcasted_iota(jnp.int32, act.shape, 1)
            act = jnp.where(col < FFN_TAIL, act, 0.0)
            row = lax.broadcasted_iota(jnp.int32, wd.shape, 0)
            wd = jnp.where(row < FFN_TAIL, wd.astype(f32), 0.0).astype(bf16)
        o_ref[...] += _dot(act.astype(bf16), wd)

    @pl.when(f < nf - 1)
    def _():
        step(False)

    @pl.when(f == nf - 1)
    def _():
        step(True)
        if final_norm:
            o_ref[...] = _rms(o_ref[...], gfin_ref[...])

            @pl.when(i == last_tile)
            def _():
                os_ref[...] = o_ref[FFN_PROMPT_TAIL:, :]


def _ffn(xs, g, wg, wu, wd, gfin, first):
    nf = FFN_NF
    tile = pl.BlockSpec((FFN_TM, D_MODEL), lambda i, f: (i, 0))
    sample = pl.BlockSpec((N_SAMPLE, D_MODEL), lambda i, f: (0, 0))
    vec = pl.BlockSpec((1, D_MODEL), lambda i, f: (0, 0))
    weights = [pl.BlockSpec((D_MODEL, FFN_TF), lambda i, f: (0, f)),
               pl.BlockSpec((D_MODEL, FFN_TF), lambda i, f: (0, f)),
               pl.BlockSpec((FFN_TF, D_MODEL), lambda i, f: (f, 0))]
    if first:
        in_specs = [tile, sample, vec] + weights + [vec]
        out_specs = tile
        out_shape = jax.ShapeDtypeStruct((N_ROWS, D_MODEL), f32)
    else:
        in_specs = [tile, vec] + weights + [vec]
        out_specs = [tile, sample]
        out_shape = [jax.ShapeDtypeStruct((N_PROMPT, D_MODEL), f32),
                     jax.ShapeDtypeStruct((N_SAMPLE, D_MODEL), f32)]
    return pl.pallas_call(
        functools.partial(_ffn_kernel, first=first, nf=nf),
        grid=(N_ROWS // FFN_TM, nf),
        in_specs=in_specs,
        out_specs=out_specs,
        out_shape=out_shape,
        scratch_shapes=[pltpu.VMEM((FFN_TM, D_MODEL), bf16)],
        compiler_params=pltpu.CompilerParams(
            dimension_semantics=("arbitrary", "arbitrary"), vmem_limit_bytes=VMEM_LIMIT),
        name="ffn_half",
    )(*xs, g, wg, wu, wd, gfin)


N_PROMPT_TILES = N_PROMPT // PROJ_TM
TILES_PER_SEQ = SEQ // PROJ_TM
assert N_SAMPLE == PROJ_TM


def _aproj_kernel(x_ref, g_ref, wq_ref, wk_ref, wv_ref, wf_ref, bf_ref,
                  qb_ref, kb_ref, kt_ref, vt_ref, vtb_ref, lf_ref, qs_ref, kts_ref, vts_ref):
    i = pl.program_id(0)
    h = _rms(x_ref[...], g_ref[...]).astype(bf16)
    q = _dot_nt(h, wq_ref[...]) * (HEAD_DIM ** -0.5)
    qb_ref[...] = q.astype(bf16)
    kb_ref[...] = _dot_nt(h, wk_ref[...]).astype(bf16)
    kt = _dot_nt(wk_ref[...], h)
    vt = _dot_nt(wv_ref[...], h)
    z = _dot_nt(h, wf_ref[...]) + bf_ref[...]
    lf_ref[...] = -(jnp.maximum(-z, 0.0) + jnp.log1p(jnp.exp(-jnp.abs(z))))

    @pl.when(i < N_PROMPT_TILES)
    def _():
        kt_ref[...] = kt
        vt_ref[...] = vt
        vtb_ref[...] = vt.astype(bf16)

    @pl.when(i == N_PROMPT_TILES)
    def _():
        qs_ref[...] = q
        kts_ref[...] = kt
        vts_ref[...] = vt


def _attn_proj(x, g, wt, wf, bfp):
    row = lambda i: (i, 0)
    fixed = lambda i: (0, 0)

    def seq_tile(i):
        t = jnp.minimum(i, N_PROMPT_TILES - 1)
        return (t // TILES_PER_SEQ, 0, t % TILES_PER_SEQ)

    wide = lambda dt: jax.ShapeDtypeStruct((N_ROWS, ATTN_WIDTH), dt)
    tposed = lambda dt: jax.ShapeDtypeStruct((BATCH, ATTN_WIDTH, SEQ), dt)
    small_t = jax.ShapeDtypeStruct((ATTN_WIDTH, N_SAMPLE), f32)
    w_block = lambda n: pl.BlockSpec((ATTN_WIDTH, D_MODEL), lambda i: (n, 0))
    return pl.pallas_call(
        _aproj_kernel,
        grid=(N_ROWS // PROJ_TM,),
        in_specs=[
            pl.BlockSpec((PROJ_TM, D_MODEL), row),
            pl.BlockSpec((1, D_MODEL), fixed),
            w_block(0), w_block(1), w_block(2),
            pl.BlockSpec((LANES, D_MODEL), fixed),
            pl.BlockSpec((1, LANES), fixed),
        ],
        out_specs=[pl.BlockSpec((PROJ_TM, ATTN_WIDTH), row)] * 2
        + [pl.BlockSpec((None, ATTN_WIDTH, PROJ_TM), seq_tile)] * 3
        + [pl.BlockSpec((PROJ_TM, LANES), row),
           pl.BlockSpec((N_SAMPLE, ATTN_WIDTH), fixed),
           pl.BlockSpec((ATTN_WIDTH, N_SAMPLE), fixed),
           pl.BlockSpec((ATTN_WIDTH, N_SAMPLE), fixed)],
        out_shape=[wide(bf16), wide(bf16), tposed(f32), tposed(f32), tposed(bf16),
                   jax.ShapeDtypeStruct((N_ROWS, LANES), f32),
                   jax.ShapeDtypeStruct((N_SAMPLE, ATTN_WIDTH), f32), small_t, small_t],
        compiler_params=pltpu.CompilerParams(
            dimension_semantics=("arbitrary",), vmem_limit_bytes=VMEM_LIMIT),
        name="attn_proj",
    )(x, g, wt, wt, wt, wf, bfp)


def _gelu_tanh(x):
    return 0.5 * x * (1.0 + jnp.tanh(0.7978845608028654 * (x + 0.044715 * (x * x * x))))


def _gmlp_kernel(x_ref, g_ref, wu_ref, wv_ref, ggv_ref, ws_ref, mask_ref, bs_ref,
                 o_ref, gvn_ref, *, n_tiles):
    i = pl.program_id(0)
    h = _rms(x_ref[...], g_ref[...]).astype(bf16)
    mask = mask_ref[...]
    bias = bs_ref[...]
    pair_w = 2 * GMLP_HEAD_DIM

    def project(gp):
        rows = slice(gp * pair_w, (gp + 1) * pair_w)
        return _dot_nt(h, wu_ref[rows, :]), _dot_nt(h, wv_ref[rows, :])

    def finish(gp, u_lin, gv_lin):
        u = _gelu_tanh(u_lin)
        gv = _gelu_tanh(gv_lin)
        for e in range(2):
            g = 2 * gp + e
            loc = slice(e * GMLP_HEAD_DIM, (e + 1) * GMLP_HEAD_DIM)
            cols = slice(g * GMLP_HEAD_DIM, (g + 1) * GMLP_HEAD_DIM)
            gvn = _rms(gv[:, loc], ggv_ref[g:g + 1, :])
            w = (ws_ref[g] * mask).astype(bf16)
            for c in range(PROJ_TM // CHUNK):
                rows = slice(c * CHUNK, (c + 1) * CHUNK)
                mixed = _dot(w, gvn[rows].astype(bf16)) + bias[:, g:g + 1]
                o_ref[rows, cols] = (u[rows, loc] * mixed).astype(bf16)

            @pl.when(i == n_tiles - 1)
            def _():
                gvn_ref[:, cols] = gvn

    n_pairs = N_GMLP_HEADS // 2
    cur = project(0)
    for gp in range(n_pairs):
        nxt = project(gp + 1) if gp + 1 < n_pairs else None
        finish(gp, *cur)
        cur = nxt


def _gmlp(x, g, wu, wv, ggv, ws2, mask2, bs2):
    n_tiles = N_ROWS // PROJ_TM
    n_prompt_tiles = N_PROMPT // PROJ_TM
    row = lambda i: (i, 0)
    fixed = lambda i: (0, 0)
    kind3 = lambda i: (i // n_prompt_tiles, 0, 0)
    kind4 = lambda i: (i // n_prompt_tiles, 0, 0, 0)
    return pl.pallas_call(
        functools.partial(_gmlp_kernel, n_tiles=n_tiles),
        grid=(n_tiles,),
        in_specs=[
            pl.BlockSpec((PROJ_TM, D_MODEL), row),
            pl.BlockSpec((1, D_MODEL), fixed),
            pl.BlockSpec((GMLP_WIDTH, D_MODEL), fixed),
            pl.BlockSpec((GMLP_WIDTH, D_MODEL), fixed),
            pl.BlockSpec((N_GMLP_HEADS, GMLP_HEAD_DIM), fixed),
            pl.BlockSpec((None, N_GMLP_HEADS, CHUNK, CHUNK), kind4),
            pl.BlockSpec((None, CHUNK, CHUNK), kind3),
            pl.BlockSpec((None, CHUNK, N_GMLP_HEADS), kind3),
        ],
        out_specs=[pl.BlockSpec((PROJ_TM, GMLP_WIDTH), row),
                   pl.BlockSpec((PROJ_TM, GMLP_WIDTH), fixed)],
        out_shape=[jax.ShapeDtypeStruct((N_ROWS, GMLP_WIDTH), bf16),
                   jax.ShapeDtypeStruct((N_SAMPLE, GMLP_WIDTH), f32)],
        compiler_params=pltpu.CompilerParams(
            dimension_semantics=("arbitrary",), vmem_limit_bytes=VMEM_LIMIT),
        name="gmlp_group",
    )(x, g, wu, wv, ggv, ws2, mask2, bs2)


def _cumsum_kernel(lf_ref, c_ref):
    r = lax.broadcasted_iota(jnp.int32, (CHUNK, CHUNK), 0)
    s = lax.broadcasted_iota(jnp.int32, (CHUNK, CHUNK), 1)
    tri = (s <= r).astype(bf16)

    def body(j, carry):
        rows = pl.ds(pl.multiple_of(j * CHUNK, CHUNK), CHUNK)
        hi, mid, lo = _split3(lf_ref[rows, :])
        c = _dot(tri, hi) + _dot(tri, mid) + _dot(tri, lo) + carry
        c_ref[rows, :] = c
        return c[CHUNK - 1:CHUNK, :]

    lax.fori_loop(0, SEQ // CHUNK, body, jnp.zeros((1, LANES), f32))


def _prompt_cumsum(lf_pad):
    return pl.pallas_call(
        _cumsum_kernel,
        grid=(BATCH,),
        in_specs=[pl.BlockSpec((SEQ, LANES), lambda b: (b, 0))],
        out_specs=pl.BlockSpec((SEQ, LANES), lambda b: (b, 0)),
        out_shape=jax.ShapeDtypeStruct((N_PROMPT, LANES), f32),
        compiler_params=pltpu.CompilerParams(
            dimension_semantics=("parallel",), vmem_limit_bytes=VMEM_LIMIT),
        name="logf_cumsum",
    )(lf_pad)


N_SPLIT = 3


def _bias_lanes(e):
    return HEAD_DIM * (1 - e)


def _fox_prompt_kernel(q_ref, k_ref, vt_ref, c_ref, o_ref,
                       kp_ref, g_ref, qp_ref, sa_ref, sb_ref, m_ref, l_ref, acc_ref):
    hp = pl.program_id(1)
    qi = pl.program_id(2)
    t = ATT_T
    lane = lax.broadcasted_iota(jnp.int32, (t, LANES), 1)

    def in_span(e, first, n):
        base = _bias_lanes(e) + first
        return (lane >= base) & (lane < base + n)

    @pl.when(qi == 0)
    def _():
        r = lax.broadcasted_iota(jnp.int32, (LANES, LANES), 0)
        l2 = lax.broadcasted_iota(jnp.int32, (LANES, LANES), 1)
        sels = []
        for j in range(N_SPLIT):
            sel = jnp.zeros((LANES, LANES), f32)
            for e in range(2):
                src = r == 2 * hp + e
                sel = sel - (src & (l2 == _bias_lanes(e) + j)).astype(f32)
                sel = sel + (src & (l2 == _bias_lanes(e) + N_SPLIT + j)).astype(f32)
            sels.append(sel.astype(bf16))

        def prep(i, _):
            rows = pl.ds(pl.multiple_of(i * t, t), t)
            parts = _split3(c_ref[rows, :])
            g = sum(_dot(p, s) for p, s in zip(parts, sels))
            g_ref[rows, :] = g
            k2 = k_ref[rows, :].astype(f32)
            for e in range(2):
                own = (lane < HEAD_DIM) if e == 0 else (lane >= HEAD_DIM)
                extra = jnp.where(in_span(e, 0, N_SPLIT), g,
                                  jnp.where(in_span(e, N_SPLIT, N_SPLIT), 1.0, 0.0))
                kp_ref[e, rows, :] = jnp.where(own, k2, extra).astype(bf16)
            return 0

        lax.fori_loop(0, SEQ // t, prep, 0)

    q0 = pl.multiple_of(qi * t, t)
    q2 = q_ref[...].astype(f32)
    gq = g_ref[pl.ds(q0, t), :]
    for e in range(2):
        own = (lane < HEAD_DIM) if e == 0 else (lane >= HEAD_DIM)
        extra = jnp.where(in_span(e, 0, N_SPLIT), 1.0,
                          jnp.where(in_span(e, N_SPLIT, N_SPLIT), gq, 0.0))
        qp_ref[e] = jnp.where(own, q2, extra).astype(bf16)

    m_ref[...] = jnp.full(m_ref.shape, NEG_INF, f32)
    l_ref[...] = jnp.zeros(l_ref.shape, f32)
    acc_ref[...] = jnp.zeros(acc_ref.shape, f32)

    def scores(blk, s_ref):
        k0 = pl.multiple_of(blk * t, t)
        for e in range(2):
            s_ref[e] = _dot_nt(kp_ref[e, pl.ds(k0, t), :], qp_ref[e])

    def consume(blk, s_ref, on_diagonal):
        k0 = pl.multiple_of(blk * t, t)
        for e in range(2):
            st = s_ref[e]
            if on_diagonal:
                key = lax.broadcasted_iota(jnp.int32, (t, t), 0)
                qry = lax.broadcasted_iota(jnp.int32, (t, t), 1)
                st = jnp.where(key <= qry, st, NEG_INF)
            m_old = m_ref[e]
            m_new = jnp.maximum(m_old, jnp.max(st, axis=0, keepdims=True))
            alpha = jnp.exp(m_old - m_new)
            pt = jnp.exp(st - m_new)
            l_ref[e] = alpha * l_ref[e] + jnp.sum(pt, axis=0, keepdims=True)
            m_ref[e] = m_new
            vt = vt_ref[e * HEAD_DIM:(e + 1) * HEAD_DIM, pl.ds(k0, t)]
            acc_ref[e] = acc_ref[e] * alpha + _dot(vt, pt.astype(bf16))

    scores(0, sa_ref)

    def pair(p, _):
        j = 2 * p
        scores(j + 1, sb_ref)
        consume(j, sa_ref, False)
        scores(j + 2, sa_ref)
        consume(j + 1, sb_ref, False)
        return 0

    lax.fori_loop(0, qi // 2, pair, 0)

    @pl.when(qi % 2 == 0)
    def _():
        consume(qi, sa_ref, True)

    @pl.when(qi % 2 == 1)
    def _():
        scores(qi, sb_ref)
        consume(qi - 1, sa_ref, False)
        consume(qi, sb_ref, True)

    for e in range(2):
        o_ref[e * HEAD_DIM:(e + 1) * HEAD_DIM, :] = (acc_ref[e] / l_ref[e]).astype(bf16)


def _fox_prompt(qb, kb, vt, c):
    nq = SEQ // ATT_T
    return pl.pallas_call(
        _fox_prompt_kernel,
        grid=(BATCH, N_HEADS // 2, nq),
        in_specs=[
            pl.BlockSpec((ATT_T, LANES), lambda b, hp, qi: (b * nq + qi, hp)),
            pl.BlockSpec((SEQ, LANES), lambda b, hp, qi: (b, hp)),
            pl.BlockSpec((None, LANES, SEQ), lambda b, hp, qi: (b, hp, 0)),
            pl.BlockSpec((SEQ, LANES), lambda b, hp, qi: (b, 0)),
        ],
        out_specs=pl.BlockSpec((LANES, ATT_T), lambda b, hp, qi: (hp, b * nq + qi)),
        out_shape=jax.ShapeDtypeStruct((ATTN_WIDTH, N_PROMPT), bf16),
        scratch_shapes=[pltpu.VMEM((2, SEQ, LANES), bf16), pltpu.VMEM((SEQ, LANES), f32),
                        pltpu.VMEM((2, ATT_T, LANES), bf16),
                        pltpu.VMEM((2, ATT_T, ATT_T), f32), pltpu.VMEM((2, ATT_T, ATT_T), f32),
                        pltpu.VMEM((2, 1, ATT_T), f32), pltpu.VMEM((2, 1, ATT_T), f32),
                        pltpu.VMEM((2, HEAD_DIM, ATT_T), f32)],
        compiler_params=pltpu.CompilerParams(
            dimension_semantics=("parallel", "parallel", "arbitrary"),
            vmem_limit_bytes=VMEM_LIMIT),
        name="fox_prompt",
    )(qb, kb, vt, c)


def _fox_sample_kernel(pt_ref, q_ref, kn_ref, vn_ref, lfn_ref, *refs):
    g = PAGES_PER_STEP
    lf_refs = refs[:g]
    k_refs = refs[g:2 * g]
    v_refs = refs[2 * g:3 * g]
    o_ref, m_ref, l_ref, acc_ref, cn_ref, carry_ref = refs[3 * g:]
    j = pl.program_id(1)
    tok = lax.broadcasted_iota(jnp.int32, (DEC_SEQ, PAGE_SIZE), 0)
    key = lax.broadcasted_iota(jnp.int32, (DEC_SEQ, PAGE_SIZE), 1)
    j_i = lax.broadcasted_iota(jnp.int32, (PAGE_SIZE, PAGE_SIZE), 0)
    s_i = lax.broadcasted_iota(jnp.int32, (PAGE_SIZE, PAGE_SIZE), 1)

    @pl.when(j == 0)
    def _():
        carry_ref[...] = jnp.zeros(carry_ref.shape, f32)
        upto = (j_i <= s_i).astype(bf16)
        c_new = _dot3(lfn_ref[...], upto)
        own = []
        for h in range(N_HEADS):
            c_row = c_new[h:h + 1, :]
            c_col = jnp.sum(jnp.where(key == tok, c_row, 0.0), axis=1, keepdims=True)
            c_b = jnp.broadcast_to(c_col, (DEC_SEQ, PAGE_SIZE))
            cn_ref[h] = c_b
            s = _dot(q_ref[h], kn_ref[h]) + (c_b - c_row)
            own.append(jnp.where(key <= tok, s, NEG_INF))
        for h in range(N_HEADS):
            s = own[h]
            m = jnp.max(s, axis=1, keepdims=True)
            p = jnp.exp(s - m)
            m_ref[h] = jnp.broadcast_to(m, (DEC_SEQ, PAGE_SIZE))
            l_ref[h] = p
            acc_ref[h] = _dot_nt(p, vn_ref[h])

    after = (j_i > s_i).astype(bf16)
    x = jnp.concatenate([lf_refs[pg][...] for pg in range(g)], axis=0)
    within = _dot3(x, after)
    total = jnp.sum(x, axis=1, keepdims=True)
    carry = carry_ref[...]
    sfx = []
    for pg in range(g):
        rows = slice(pg * N_HEADS, (pg + 1) * N_HEADS)
        sfx.append(within[rows] + carry)
        carry = carry + total[rows]
    carry_ref[...] = carry

    scores, row_max = [], []
    for h in range(N_HEADS):
        q = q_ref[h]
        bias = cn_ref[h]
        s = [_dot(q, k_refs[pg][h]) + (bias + sfx[pg][h:h + 1, :]) for pg in range(g)]
        mx = s[0]
        for pg in range(1, g):
            mx = jnp.maximum(mx, s[pg])
        scores.append(s)
        row_max.append(jnp.max(mx, axis=1, keepdims=True))
    for h in range(N_HEADS):
        s = scores[h]
        m_old = m_ref[h]
        m_new = jnp.maximum(m_old, row_max[h])
        alpha = jnp.exp(m_old - m_new)
        p = [jnp.exp(s[pg] - m_new) for pg in range(g)]
        l_ref[h] = alpha * l_ref[h] + sum(p)
        m_ref[h] = m_new
        pv = sum(_dot_nt(p[pg], v_refs[pg][h]) for pg in range(g))
        acc_ref[h] = acc_ref[h] * alpha[:, :HEAD_DIM] + pv

    @pl.when(j == pl.num_programs(1) - 1)
    def _():
        for h in range(N_HEADS):
            o_ref[h] = acc_ref[h] / jnp.sum(l_ref[h], axis=1, keepdims=True)


def _fox_sample(page_table, q4, kn_t, vn_t, lfn_t, cache_lft, cache_kt, cache_vt):
    g = PAGES_PER_STEP
    nj = N_PAGES // g

    def page_of(b, j, pt, i):
        return pt[b, N_PAGES - 1 - (j * g + i)]

    per_b4 = lambda b, j, pt: (b, 0, 0, 0)
    page_spec = lambda i: pl.BlockSpec((None, N_HEADS, HEAD_DIM, PAGE_SIZE),
                                       lambda b, j, pt: (page_of(b, j, pt, i), 0, 0, 0))
    lf_spec = lambda i: pl.BlockSpec((None, N_HEADS, PAGE_SIZE),
                                     lambda b, j, pt: (page_of(b, j, pt, i), 0, 0))
    state = pltpu.VMEM((N_HEADS, DEC_SEQ, PAGE_SIZE), f32)
    grid_spec = pltpu.PrefetchScalarGridSpec(
        num_scalar_prefetch=1,
        grid=(DEC_BATCH, nj),
        in_specs=[
            pl.BlockSpec((None, N_HEADS, DEC_SEQ, HEAD_DIM), per_b4),
            pl.BlockSpec((None, N_HEADS, HEAD_DIM, PAGE_SIZE), per_b4),
            pl.BlockSpec((None, N_HEADS, HEAD_DIM, PAGE_SIZE), per_b4),
            pl.BlockSpec((None, N_HEADS, PAGE_SIZE), lambda b, j, pt: (b, 0, 0)),
        ] + [lf_spec(i) for i in range(g)] + [page_spec(i) for i in range(g)]
        + [page_spec(i) for i in range(g)],
        out_specs=pl.BlockSpec((None, N_HEADS, DEC_SEQ, HEAD_DIM), per_b4),
        scratch_shapes=[state, state, pltpu.VMEM((N_HEADS, DEC_SEQ, HEAD_DIM), f32), state,
                        pltpu.VMEM((N_HEADS, PAGE_SIZE), f32)],
    )
    return pl.pallas_call(
        _fox_sample_kernel,
        grid_spec=grid_spec,
        out_shape=jax.ShapeDtypeStruct((DEC_BATCH, N_HEADS, DEC_SEQ, HEAD_DIM), f32),
        compiler_params=pltpu.CompilerParams(
            dimension_semantics=("parallel", "arbitrary"), vmem_limit_bytes=VMEM_LIMIT),
        name="fox_sample",
    )(page_table, q4, kn_t, vn_t, lfn_t, *([cache_lft] * g), *([cache_kt] * g), *([cache_vt] * g))


def _oproj_kernel(x_ref, at_ref, as_ref, m_ref, wo_ref, o_ref):
    i = pl.program_id(0)
    wa = wo_ref[0:ATTN_WIDTH, :]
    base = x_ref[...] + _dot(m_ref[...], wo_ref[ATTN_WIDTH:, :])

    @pl.when(i < N_PROMPT_TILES)
    def _():
        o_ref[...] = base + lax.dot_general(at_ref[...], wa, (((0,), (0,)), ((), ())),
                                            preferred_element_type=f32)

    @pl.when(i == N_PROMPT_TILES)
    def _():
        o_ref[...] = base + _dot(as_ref[...], wa)


def _out_proj(x, att_t, att_s, gm, wo):
    row = lambda i: (i, 0)
    fixed = lambda i: (0, 0)
    return pl.pallas_call(
        _oproj_kernel,
        grid=(N_ROWS // PROJ_TM,),
        in_specs=[
            pl.BlockSpec((PROJ_TM, D_MODEL), row),
            pl.BlockSpec((ATTN_WIDTH, PROJ_TM), lambda i: (0, jnp.minimum(i, N_PROMPT_TILES - 1))),
            pl.BlockSpec((N_SAMPLE, ATTN_WIDTH), fixed),
            pl.BlockSpec((PROJ_TM, GMLP_WIDTH), row),
            pl.BlockSpec((ATTN_WIDTH + GMLP_WIDTH, D_MODEL), fixed),
        ],
        out_specs=pl.BlockSpec((PROJ_TM, D_MODEL), row),
        out_shape=jax.ShapeDtypeStruct((N_ROWS, D_MODEL), f32),
        compiler_params=pltpu.CompilerParams(
            dimension_semantics=("arbitrary",), vmem_limit_bytes=VMEM_LIMIT),
        name="out_proj",
    )(x, att_t, att_s, gm, wo)


def kernel(x_prompt, x_sample, cache_k, cache_v, cache_logf, page_table, g_ffn1, w1_gate, w1_up, w1_down, g_mix, w_in, b_f, g_gv, w_s, b_s, w_out, g_ffn2, w2_gate, w2_up, w2_down, g_final):
    l = 0
    x_in = (x_prompt.reshape(N_PROMPT, D_MODEL), x_sample.reshape(N_SAMPLE, D_MODEL))
    g_fin = g_final.reshape(1, D_MODEL)

    w1g, w1u, w1d = w1_gate[l].astype(bf16), w1_up[l].astype(bf16), w1_down[l].astype(bf16)
    w2g, w2u, w2d = w2_gate[l].astype(bf16), w2_up[l].astype(bf16), w2_down[l].astype(bf16)
    wt = w_in[l].T.astype(bf16)
    a = ATTN_WIDTH
    wf = jnp.pad(wt[3 * a:3 * a + N_HEADS], ((0, LANES - N_HEADS), (0, 0)))
    wu = wt[3 * a + N_HEADS:3 * a + N_HEADS + GMLP_WIDTH]
    wgv = wt[3 * a + N_HEADS + GMLP_WIDTH:]
    bfp = jnp.pad(b_f[l].reshape(1, N_HEADS), ((0, 0), (0, LANES - N_HEADS)))
    wo = w_out[l].astype(bf16)

    reps = CHUNK // DEC_SEQ
    ws2 = jnp.stack([w_s[l], jnp.tile(w_s[l][:, :DEC_SEQ, :DEC_SEQ], (1, reps, reps))])
    ti = jnp.arange(CHUNK)
    tril = (ti[None, :] <= ti[:, None])
    same = (ti[None, :] // DEC_SEQ) == (ti[:, None] // DEC_SEQ)
    mask2 = jnp.stack([tril, tril & same]).astype(f32)
    bs2 = jnp.stack([b_s[l].T, jnp.tile(b_s[l][:, :DEC_SEQ], (1, reps)).T])

    x1 = _ffn(x_in, g_ffn1[l].reshape(1, D_MODEL), w1g, w1u, w1d, g_fin, True)

    gm = g_mix[l].reshape(1, D_MODEL)
    qb, kb, kt_p, vt_p, vtb, lf_pad, q_s, kt_s, vt_s = _attn_proj(x1, gm, wt, wf, bfp)
    gmlp_out, gvn_s = _gmlp(x1, gm, wu, wgv, g_gv[l], ws2, mask2, bs2)

    c = _prompt_cumsum(lf_pad)
    att_t = _fox_prompt(qb, kb, vtb, c)

    def sample_keys_last(xt):
        return xt.reshape(N_HEADS, HEAD_DIM, DEC_BATCH, DEC_SEQ).transpose(2, 0, 1, 3)

    def as_page(x4):
        return jnp.pad(x4, ((0, 0), (0, 0), (0, 0), (0, PAGE_SIZE - DEC_SEQ)))

    k_s4, v_s4 = sample_keys_last(kt_s), sample_keys_last(vt_s)
    q4 = q_s.reshape(DEC_BATCH, DEC_SEQ, N_HEADS, HEAD_DIM).transpose(0, 2, 1, 3)
    lf = lf_pad[:, :N_HEADS]
    lfn_t = jnp.pad(lf[N_PROMPT:].reshape(DEC_BATCH, DEC_SEQ, N_HEADS).transpose(0, 2, 1),
                    ((0, 0), (0, 0), (0, PAGE_SIZE - DEC_SEQ)))
    cache_kt = cache_k[l].transpose(0, 2, 3, 1)
    cache_vt = cache_v[l].transpose(0, 2, 3, 1)
    cache_lft = cache_logf[l].transpose(0, 2, 1)
    att_s = _fox_sample(page_table, q4, as_page(k_s4), as_page(v_s4), lfn_t, cache_lft, cache_kt, cache_vt)
    att_s = att_s.transpose(0, 2, 1, 3).reshape(N_SAMPLE, ATTN_WIDTH).astype(bf16)

    x2 = _out_proj(x1, att_t, att_s, gmlp_out, wo)
    y_p, y_s = _ffn((x2,), g_ffn2[l].reshape(1, D_MODEL), w2g, w2u, w2d, g_fin, False)

    def prompt_rows(xt):
        return xt.reshape(BATCH, N_HEADS, HEAD_DIM, SEQ).transpose(0, 3, 1, 2)[None]

    def sample_rows(x4):
        return x4.transpose(0, 3, 1, 2)[None]

    return (y_p.reshape(BATCH, SEQ, D_MODEL),
            y_s.reshape(DEC_BATCH, DEC_SEQ, D_MODEL),
            prompt_rows(kt_p), prompt_rows(vt_p),
            lf[:N_PROMPT].reshape(1, BATCH, SEQ, N_HEADS),
            sample_rows(k_s4), sample_rows(v_s4),
            lf[N_PROMPT:].reshape(1, DEC_BATCH, DEC_SEQ, N_HEADS),
            gvn_s.reshape(1, DEC_BATCH, DEC_SEQ, N_GMLP_HEADS, GMLP_HEAD_DIM))
```

```python
import functools

import jax
import jax.numpy as jnp
from jax import lax
from jax.experimental import pallas as pl
from jax.experimental.pallas import tpu as pltpu

f32 = jnp.float32
bf16 = jnp.bfloat16

D_MODEL = 2048
BATCH = 2
SEQ = 4096
DEC_BATCH = 32
DEC_SEQ = 8
PAST_LEN = 16384
PAGE_SIZE = 128
ATTN_WIDTH = 1024
GMLP_WIDTH = 1024
HEAD_DIM = 64
N_HEADS = 16
CHUNK = 128
GMLP_HEAD_DIM = 128
N_GMLP_HEADS = 8
D_FF = 5504
EPS = 1e-6
NEG_INF = -1e30
N_PAGES = PAST_LEN // PAGE_SIZE

LANES = 128
SUBLANES = 8
MXU_DIM = 256

N_PROMPT = BATCH * SEQ
N_SAMPLE = DEC_BATCH * DEC_SEQ
N_ROWS = N_PROMPT + N_SAMPLE

FFN_TM = 768
FFN_TF = 512
FFN_NF = -(-D_FF // FFN_TF)
FFN_TAIL = D_FF - (FFN_NF - 1) * FFN_TF
FFN_PROMPT_TAIL = N_PROMPT % FFN_TM
assert FFN_PROMPT_TAIL + N_SAMPLE == FFN_TM and FFN_PROMPT_TAIL % SUBLANES == 0
PROJ_TM = 256
ATT_T = 512
ATT_SUM_ROWS = 2 * SUBLANES
LOG2E = 1.4426950408889634
PAGES_PER_STEP = 16
VMEM_LIMIT = 56 * 1024 * 1024


def _rms(x, g):
    return x * lax.rsqrt(jnp.mean(x * x, axis=-1, keepdims=True) + EPS) * g


def _split3(x):
    hi = x.astype(bf16)
    r1 = x - hi.astype(f32)
    mid = r1.astype(bf16)
    lo = (r1 - mid.astype(f32)).astype(bf16)
    return hi, mid, lo


def _dot(a, b):
    return jnp.dot(a, b, preferred_element_type=f32)


def _dot_nt(a, b):
    return lax.dot_general(a, b, (((1,), (1,)), ((), ())), preferred_element_type=f32)


def _dot3(x, sel):
    hi, mid, lo = _split3(x)
    return _dot(hi, sel) + _dot(mid, sel) + _dot(lo, sel)


def _ffn_kernel(*refs, first, nf):
    if first:
        x_ref, xs_ref, g_ref, wg_ref, wu_ref, wd_ref, gfin_ref, o_ref, h_ref = refs
    else:
        x_ref, g_ref, wg_ref, wu_ref, wd_ref, gfin_ref, o_ref, os_ref, h_ref = refs
    i = pl.program_id(0)
    f = pl.program_id(1)
    last_tile = pl.num_programs(0) - 1
    final_norm = not first

    def start(x):
        h_ref[...] = _rms(x, g_ref[...]).astype(bf16)
        o_ref[...] = x

    if first:
        @pl.when((f == 0) & (i < last_tile))
        def _():
            start(x_ref[...])

        @pl.when((f == 0) & (i == last_tile))
        def _():
            start(jnp.concatenate([x_ref[0:FFN_PROMPT_TAIL, :], xs_ref[...]], axis=0))
    else:
        @pl.when(f == 0)
        def _():
            start(x_ref[...])

    def step(tail):
        h = h_ref[...]
        gate = _dot(h, wg_ref[...])
        up = _dot(h, wu_ref[...])
        act = (gate / (1.0 + jnp.exp(-gate))) * up * 0.5
        wd = wd_ref[...]
        if tail:
            col = lax.broadcasted_iota(jnp.int32, act.shape, 1)
            act = jnp.where(col < FFN_TAIL, act, 0.0)
            row = lax.broadcasted_iota(jnp.int32, wd.shape, 0)
            wd = jnp.where(row < FFN_TAIL, wd.astype(f32), 0.0).astype(bf16)
        o_ref[...] += _dot(act.astype(bf16), wd)

    @pl.when(f < nf - 1)
    def _():
        step(False)

    @pl.when(f == nf - 1)
    def _():
        step(True)
        if final_norm:
            o_ref[...] = _rms(o_ref[...], gfin_ref[...])

            @pl.when(i == last_tile)
            def _():
                os_ref[...] = o_ref[FFN_PROMPT_TAIL:, :]


def _ffn(xs, g, wg, wu, wd, gfin, first):
    nf = FFN_NF
    tile = pl.BlockSpec((FFN_TM, D_MODEL), lambda i, f: (i, 0))
    sample = pl.BlockSpec((N_SAMPLE, D_MODEL), lambda i, f: (0, 0))
    vec = pl.BlockSpec((1, D_MODEL), lambda i, f: (0, 0))
    weights = [pl.BlockSpec((D_MODEL, FFN_TF), lambda i, f: (0, f)),
               pl.BlockSpec((D_MODEL, FFN_TF), lambda i, f: (0, f)),
               pl.BlockSpec((FFN_TF, D_MODEL), lambda i, f: (f, 0))]
    if first:
        in_specs = [tile, sample, vec] + weights + [vec]
        out_specs = tile
        out_shape = jax.ShapeDtypeStruct((N_ROWS, D_MODEL), f32)
    else:
        in_specs = [tile, vec] + weights + [vec]
        out_specs = [tile, sample]
        out_shape = [jax.ShapeDtypeStruct((N_PROMPT, D_MODEL), f32),
                     jax.ShapeDtypeStruct((N_SAMPLE, D_MODEL), f32)]
    return pl.pallas_call(
        functools.partial(_ffn_kernel, first=first, nf=nf),
        grid=(N_ROWS // FFN_TM, nf),
        in_specs=in_specs,
        out_specs=out_specs,
        out_shape=out_shape,
        scratch_shapes=[pltpu.VMEM((FFN_TM, D_MODEL), bf16)],
        compiler_params=pltpu.CompilerParams(
            dimension_semantics=("arbitrary", "arbitrary"), vmem_limit_bytes=VMEM_LIMIT),
        name="ffn_half",
    )(*xs, g, wg, wu, wd, gfin)


N_PROMPT_TILES = N_PROMPT // PROJ_TM
TILES_PER_SEQ = SEQ // PROJ_TM
assert N_SAMPLE == PROJ_TM


def _aproj_kernel(x_ref, g_ref, wq_ref, wk_ref, wv_ref, wf_ref, bf_ref,
                  qb_ref, kb_ref, kt_ref, vt_ref, vtb_ref, lf_ref, qs_ref, kts_ref, vts_ref):
    i = pl.program_id(0)
    h = _rms(x_ref[...], g_ref[...]).astype(bf16)
    q = _dot_nt(h, wq_ref[...]) * (HEAD_DIM ** -0.5)
    qb_ref[...] = (q * LOG2E).astype(bf16)
    kb_ref[...] = _dot_nt(h, wk_ref[...]).astype(bf16)
    kt = _dot_nt(wk_ref[...], h)
    vt = _dot_nt(wv_ref[...], h)
    z = _dot_nt(h, wf_ref[...]) + bf_ref[...]
    lf_ref[...] = -(jnp.maximum(-z, 0.0) + jnp.log1p(jnp.exp(-jnp.abs(z))))

    @pl.when(i < N_PROMPT_TILES)
    def _():
        kt_ref[...] = kt
        vt_ref[...] = vt
        vtb_ref[...] = vt.astype(bf16)

    @pl.when(i == N_PROMPT_TILES)
    def _():
        qs_ref[...] = q
        kts_ref[...] = kt
        vts_ref[...] = vt


def _attn_proj(x, g, wt, wf, bfp):
    row = lambda i: (i, 0)
    fixed = lambda i: (0, 0)

    def seq_tile(i):
        t = jnp.minimum(i, N_PROMPT_TILES - 1)
        return (t // TILES_PER_SEQ, 0, t % TILES_PER_SEQ)

    wide = lambda dt: jax.ShapeDtypeStruct((N_ROWS, ATTN_WIDTH), dt)
    tposed = lambda dt: jax.ShapeDtypeStruct((BATCH, ATTN_WIDTH, SEQ), dt)
    small_t = jax.ShapeDtypeStruct((ATTN_WIDTH, N_SAMPLE), f32)
    w_block = lambda n: pl.BlockSpec((ATTN_WIDTH, D_MODEL), lambda i: (n, 0))
    return pl.pallas_call(
        _aproj_kernel,
        grid=(N_ROWS // PROJ_TM,),
        in_specs=[
            pl.BlockSpec((PROJ_TM, D_MODEL), row),
            pl.BlockSpec((1, D_MODEL), fixed),
            w_block(0), w_block(1), w_block(2),
            pl.BlockSpec((LANES, D_MODEL), fixed),
            pl.BlockSpec((1, LANES), fixed),
        ],
        out_specs=[pl.BlockSpec((PROJ_TM, ATTN_WIDTH), row)] * 2
        + [pl.BlockSpec((None, ATTN_WIDTH, PROJ_TM), seq_tile)] * 3
        + [pl.BlockSpec((PROJ_TM, LANES), row),
           pl.BlockSpec((N_SAMPLE, ATTN_WIDTH), fixed),
           pl.BlockSpec((ATTN_WIDTH, N_SAMPLE), fixed),
           pl.BlockSpec((ATTN_WIDTH, N_SAMPLE), fixed)],
        out_shape=[wide(bf16), wide(bf16), tposed(f32), tposed(f32), tposed(bf16),
                   jax.ShapeDtypeStruct((N_ROWS, LANES), f32),
                   jax.ShapeDtypeStruct((N_SAMPLE, ATTN_WIDTH), f32), small_t, small_t],
        compiler_params=pltpu.CompilerParams(
            dimension_semantics=("arbitrary",), vmem_limit_bytes=VMEM_LIMIT),
        name="attn_proj",
    )(x, g, wt, wt, wt, wf, bfp)


def _gelu_tanh(x):
    return 0.5 * x * (1.0 + jnp.tanh(0.7978845608028654 * (x + 0.044715 * (x * x * x))))


def _gmlp_kernel(x_ref, g_ref, wu_ref, wv_ref, ggv_ref, ws_ref, mask_ref, bs_ref,
                 o_ref, gvn_ref, *, n_tiles):
    i = pl.program_id(0)
    h = _rms(x_ref[...], g_ref[...]).astype(bf16)
    mask = mask_ref[...]
    bias = bs_ref[...]
    pair_w = 2 * GMLP_HEAD_DIM

    def project(gp):
        rows = slice(gp * pair_w, (gp + 1) * pair_w)
        return _dot_nt(h, wu_ref[rows, :]), _dot_nt(h, wv_ref[rows, :])

    def finish(gp, u_lin, gv_lin):
        u = _gelu_tanh(u_lin)
        gv = _gelu_tanh(gv_lin)
        for e in range(2):
            g = 2 * gp + e
            loc = slice(e * GMLP_HEAD_DIM, (e + 1) * GMLP_HEAD_DIM)
            cols = slice(g * GMLP_HEAD_DIM, (g + 1) * GMLP_HEAD_DIM)
            gvn = _rms(gv[:, loc], ggv_ref[g:g + 1, :])
            w = (ws_ref[g] * mask).astype(bf16)
            for c in range(PROJ_TM // CHUNK):
                rows = slice(c * CHUNK, (c + 1) * CHUNK)
                mixed = _dot(w, gvn[rows].astype(bf16)) + bias[:, g:g + 1]
                o_ref[rows, cols] = (u[rows, loc] * mixed).astype(bf16)

            @pl.when(i == n_tiles - 1)
            def _():
                gvn_ref[:, cols] = gvn

    n_pairs = N_GMLP_HEADS // 2
    cur = project(0)
    for gp in range(n_pairs):
        nxt = project(gp + 1) if gp + 1 < n_pairs else None
        finish(gp, *cur)
        cur = nxt


def _gmlp(x, g, wu, wv, ggv, ws2, mask2, bs2):
    n_tiles = N_ROWS // PROJ_TM
    n_prompt_tiles = N_PROMPT // PROJ_TM
    row = lambda i: (i, 0)
    fixed = lambda i: (0, 0)
    kind3 = lambda i: (i // n_prompt_tiles, 0, 0)
    kind4 = lambda i: (i // n_prompt_tiles, 0, 0, 0)
    return pl.pallas_call(
        functools.partial(_gmlp_kernel, n_tiles=n_tiles),
        grid=(n_tiles,),
        in_specs=[
            pl.BlockSpec((PROJ_TM, D_MODEL), row),
            pl.BlockSpec((1, D_MODEL), fixed),
            pl.BlockSpec((GMLP_WIDTH, D_MODEL), fixed),
            pl.BlockSpec((GMLP_WIDTH, D_MODEL), fixed),
            pl.BlockSpec((N_GMLP_HEADS, GMLP_HEAD_DIM), fixed),
            pl.BlockSpec((None, N_GMLP_HEADS, CHUNK, CHUNK), kind4),
            pl.BlockSpec((None, CHUNK, CHUNK), kind3),
            pl.BlockSpec((None, CHUNK, N_GMLP_HEADS), kind3),
        ],
        out_specs=[pl.BlockSpec((PROJ_TM, GMLP_WIDTH), row),
                   pl.BlockSpec((PROJ_TM, GMLP_WIDTH), fixed)],
        out_shape=[jax.ShapeDtypeStruct((N_ROWS, GMLP_WIDTH), bf16),
                   jax.ShapeDtypeStruct((N_SAMPLE, GMLP_WIDTH), f32)],
        compiler_params=pltpu.CompilerParams(
            dimension_semantics=("arbitrary",), vmem_limit_bytes=VMEM_LIMIT),
        name="gmlp_group",
    )(x, g, wu, wv, ggv, ws2, mask2, bs2)


def _cumsum_kernel(lf_ref, c_ref):
    r = lax.broadcasted_iota(jnp.int32, (CHUNK, CHUNK), 0)
    s = lax.broadcasted_iota(jnp.int32, (CHUNK, CHUNK), 1)
    tri = (s <= r).astype(bf16)

    def body(j, carry):
        rows = pl.ds(pl.multiple_of(j * CHUNK, CHUNK), CHUNK)
        hi, mid, lo = _split3(lf_ref[rows, :])
        c = _dot(tri, hi) + _dot(tri, mid) + _dot(tri, lo) + carry
        c_ref[rows, :] = c
        return c[CHUNK - 1:CHUNK, :]

    lax.fori_loop(0, SEQ // CHUNK, body, jnp.zeros((1, LANES), f32))


def _prompt_cumsum(lf_pad):
    return pl.pallas_call(
        _cumsum_kernel,
        grid=(BATCH,),
        in_specs=[pl.BlockSpec((SEQ, LANES), lambda b: (b, 0))],
        out_specs=pl.BlockSpec((SEQ, LANES), lambda b: (b, 0)),
        out_shape=jax.ShapeDtypeStruct((N_PROMPT, LANES), f32),
        compiler_params=pltpu.CompilerParams(
            dimension_semantics=("parallel",), vmem_limit_bytes=VMEM_LIMIT),
        name="logf_cumsum",
    )(lf_pad)


N_SPLIT = 3


def _bias_lanes(e):
    return HEAD_DIM * (1 - e)


def _fox_prompt_kernel(q_ref, k_ref, vt_ref, c_ref, o_ref,
                       kp_ref, g_ref, qp_ref, sa_ref, sb_ref, mxa_ref, mxb_ref, m_ref, acc_ref):
    hp = pl.program_id(1)
    qi = pl.program_id(2)
    t = ATT_T
    lane = lax.broadcasted_iota(jnp.int32, (t, LANES), 1)

    def in_span(e, first, n):
        base = _bias_lanes(e) + first
        return (lane >= base) & (lane < base + n)

    @pl.when(qi == 0)
    def _():
        r = lax.broadcasted_iota(jnp.int32, (LANES, LANES), 0)
        l2 = lax.broadcasted_iota(jnp.int32, (LANES, LANES), 1)
        sels = []
        for j in range(N_SPLIT):
            sel = jnp.zeros((LANES, LANES), f32)
            for e in range(2):
                src = r == 2 * hp + e
                sel = sel - (src & (l2 == _bias_lanes(e) + j)).astype(f32)
                sel = sel + (src & (l2 == _bias_lanes(e) + N_SPLIT + j)).astype(f32)
            sels.append(sel.astype(bf16))

        def prep(i, _):
            rows = pl.ds(pl.multiple_of(i * t, t), t)
            parts = _split3(c_ref[rows, :] * LOG2E)
            g = sum(_dot(p, s) for p, s in zip(parts, sels))
            g_ref[rows, :] = g
            k2 = k_ref[rows, :].astype(f32)
            for e in range(2):
                own = (lane < HEAD_DIM) if e == 0 else (lane >= HEAD_DIM)
                extra = jnp.where(in_span(e, 0, N_SPLIT), g,
                                  jnp.where(in_span(e, N_SPLIT, N_SPLIT), 1.0, 0.0))
                kp_ref[e, rows, :] = jnp.where(own, k2, extra).astype(bf16)
            return 0

        lax.fori_loop(0, SEQ // t, prep, 0)

    q0 = pl.multiple_of(qi * t, t)
    q2 = q_ref[...].astype(f32)
    gq = g_ref[pl.ds(q0, t), :]
    for e in range(2):
        own = (lane < HEAD_DIM) if e == 0 else (lane >= HEAD_DIM)
        extra = jnp.where(in_span(e, 0, N_SPLIT), 1.0,
                          jnp.where(in_span(e, N_SPLIT, N_SPLIT), gq, 0.0))
        qp_ref[e] = jnp.where(own, q2, extra).astype(bf16)

    m_ref[...] = jnp.full(m_ref.shape, NEG_INF, f32)
    acc_ref[...] = jnp.zeros(acc_ref.shape, f32)
    ones = jnp.ones((ATT_SUM_ROWS, t), bf16)

    def scores(blk, s_ref, mx_ref):
        k0 = pl.multiple_of(blk * t, t)
        for e in range(2):
            st = _dot_nt(kp_ref[e, pl.ds(k0, t), :], qp_ref[e])
            s_ref[e] = st
            mx_ref[e] = jnp.max(st, axis=0, keepdims=True)

    def consume(blk, s_ref, mx_ref, on_diagonal):
        k0 = pl.multiple_of(blk * t, t)
        for e in range(2):
            st = s_ref[e]
            if on_diagonal:
                key = lax.broadcasted_iota(jnp.int32, (t, t), 0)
                qry = lax.broadcasted_iota(jnp.int32, (t, t), 1)
                st = jnp.where(key <= qry, st, NEG_INF)
                mx = jnp.max(st, axis=0, keepdims=True)
            else:
                mx = mx_ref[e]
            m_old = m_ref[e]
            m_new = jnp.maximum(m_old, mx)
            alpha = jnp.exp2(m_old - m_new)
            pt = jnp.exp2(st - m_new)
            m_ref[e] = m_new
            vt = jnp.concatenate([vt_ref[e * HEAD_DIM:(e + 1) * HEAD_DIM, pl.ds(k0, t)], ones], axis=0)
            acc_ref[e] = acc_ref[e] * alpha + _dot(vt, pt.astype(bf16))

    scores(0, sa_ref, mxa_ref)

    def pair(p, _):
        j = 2 * p
        scores(j + 1, sb_ref, mxb_ref)
        consume(j, sa_ref, mxa_ref, False)
        scores(j + 2, sa_ref, mxa_ref)
        consume(j + 1, sb_ref, mxb_ref, False)
        return 0

    lax.fori_loop(0, qi // 2, pair, 0)

    @pl.when(qi % 2 == 0)
    def _():
        consume(qi, sa_ref, mxa_ref, True)

    @pl.when(qi % 2 == 1)
    def _():
        scores(qi, sb_ref, mxb_ref)
        consume(qi - 1, sa_ref, mxa_ref, False)
        consume(qi, sb_ref, mxb_ref, True)

    for e in range(2):
        acc = acc_ref[e]
        o_ref[e * HEAD_DIM:(e + 1) * HEAD_DIM, :] = (
            acc[0:HEAD_DIM] / acc[HEAD_DIM:HEAD_DIM + 1]).astype(bf16)


def _fox_prompt(qb, kb, vt, c):
    nq = SEQ // ATT_T
    return pl.pallas_call(
        _fox_prompt_kernel,
        grid=(BATCH, N_HEADS // 2, nq),
        in_specs=[
            pl.BlockSpec((ATT_T, LANES), lambda b, hp, qi: (b * nq + qi, hp)),
            pl.BlockSpec((SEQ, LANES), lambda b, hp, qi: (b, hp)),
            pl.BlockSpec((None, LANES, SEQ), lambda b, hp, qi: (b, hp, 0)),
            pl.BlockSpec((SEQ, LANES), lambda b, hp, qi: (b, 0)),
        ],
        out_specs=pl.BlockSpec((LANES, ATT_T), lambda b, hp, qi: (hp, b * nq + qi)),
        out_shape=jax.ShapeDtypeStruct((ATTN_WIDTH, N_PROMPT), bf16),
        scratch_shapes=[pltpu.VMEM((2, SEQ, LANES), bf16), pltpu.VMEM((SEQ, LANES), f32),
                        pltpu.VMEM((2, ATT_T, LANES), bf16),
                        pltpu.VMEM((2, ATT_T, ATT_T), f32), pltpu.VMEM((2, ATT_T, ATT_T), f32),
                        pltpu.VMEM((2, 1, ATT_T), f32), pltpu.VMEM((2, 1, ATT_T), f32),
                        pltpu.VMEM((2, 1, ATT_T), f32),
                        pltpu.VMEM((2, HEAD_DIM + ATT_SUM_ROWS, ATT_T), f32)],
        compiler_params=pltpu.CompilerParams(
            dimension_semantics=("parallel", "parallel", "arbitrary"),
            vmem_limit_bytes=VMEM_LIMIT),
        name="fox_prompt",
    )(qb, kb, vt, c)


def _fox_sample_kernel(pt_ref, q_ref, kn_ref, vn_ref, lfn_ref, *refs):
    g = PAGES_PER_STEP
    lf_refs = refs[:g]
    k_refs = refs[g:2 * g]
    v_refs = refs[2 * g:3 * g]
    o_ref, m_ref, l_ref, acc_ref, cn_ref, carry_ref = refs[3 * g:]
    j = pl.program_id(1)
    tok = lax.broadcasted_iota(jnp.int32, (DEC_SEQ, PAGE_SIZE), 0)
    key = lax.broadcasted_iota(jnp.int32, (DEC_SEQ, PAGE_SIZE), 1)
    j_i = lax.broadcasted_iota(jnp.int32, (PAGE_SIZE, PAGE_SIZE), 0)
    s_i = lax.broadcasted_iota(jnp.int32, (PAGE_SIZE, PAGE_SIZE), 1)

    @pl.when(j == 0)
    def _():
        carry_ref[...] = jnp.zeros(carry_ref.shape, f32)
        upto = (j_i <= s_i).astype(bf16)
        c_new = _dot3(lfn_ref[...], upto)
        own = []
        for h in range(N_HEADS):
            c_row = c_new[h:h + 1, :]
            c_col = jnp.sum(jnp.where(key == tok, c_row, 0.0), axis=1, keepdims=True)
            c_b = jnp.broadcast_to(c_col, (DEC_SEQ, PAGE_SIZE))
            cn_ref[h] = c_b
            s = _dot(q_ref[h], kn_ref[h]) + (c_b - c_row)
            own.append(jnp.where(key <= tok, s, NEG_INF))
        for h in range(N_HEADS):
            s = own[h]
            m = jnp.max(s, axis=1, keepdims=True)
            p = jnp.exp(s - m)
            m_ref[h] = jnp.broadcast_to(m, (DEC_SEQ, PAGE_SIZE))
            l_ref[h] = p
            acc_ref[h] = _dot_nt(p, vn_ref[h])

    after = (j_i > s_i).astype(bf16)
    x = jnp.concatenate([lf_refs[pg][...] for pg in range(g)], axis=0)
    within = _dot3(x, after)
    total = jnp.sum(x, axis=1, keepdims=True)
    carry = carry_ref[...]
    sfx = []
    for pg in range(g):
        rows = slice(pg * N_HEADS, (pg + 1) * N_HEADS)
        sfx.append(within[rows] + carry)
        carry = carry + total[rows]
    carry_ref[...] = carry

    scores, row_max = [], []
    for h in range(N_HEADS):
        q = q_ref[h]
        bias = cn_ref[h]
        s = [_dot(q, k_refs[pg][h]) + (bias + sfx[pg][h:h + 1, :]) for pg in range(g)]
        mx = s[0]
        for pg in range(1, g):
            mx = jnp.maximum(mx, s[pg])
        scores.append(s)
        row_max.append(jnp.max(mx, axis=1, keepdims=True))
    for h in range(N_HEADS):
        s = scores[h]
        m_old = m_ref[h]
        m_new = jnp.maximum(m_old, row_max[h])
        alpha = jnp.exp(m_old - m_new)
        p = [jnp.exp(s[pg] - m_new) for pg in range(g)]
        l_ref[h] = alpha * l_ref[h] + sum(p)
        m_ref[h] = m_new
        pv = sum(_dot_nt(p[pg], v_refs[pg][h]) for pg in range(g))
        acc_ref[h] = acc_ref[h] * alpha[:, :HEAD_DIM] + pv

    @pl.when(j == pl.num_programs(1) - 1)
    def _():
        for h in range(N_HEADS):
            o_ref[h] = acc_ref[h] / jnp.sum(l_ref[h], axis=1, keepdims=True)


def _fox_sample(page_table, q4, kn_t, vn_t, lfn_t, cache_lft, cache_kt, cache_vt):
    g = PAGES_PER_STEP
    nj = N_PAGES // g

    def page_of(b, j, pt, i):
        return pt[b, N_PAGES - 1 - (j * g + i)]

    per_b4 = lambda b, j, pt: (b, 0, 0, 0)
    page_spec = lambda i: pl.BlockSpec((None, N_HEADS, HEAD_DIM, PAGE_SIZE),
                                       lambda b, j, pt: (page_of(b, j, pt, i), 0, 0, 0))
    lf_spec = lambda i: pl.BlockSpec((None, N_HEADS, PAGE_SIZE),
                                     lambda b, j, pt: (page_of(b, j, pt, i), 0, 0))
    state = pltpu.VMEM((N_HEADS, DEC_SEQ, PAGE_SIZE), f32)
    grid_spec = pltpu.PrefetchScalarGridSpec(
        num_scalar_prefetch=1,
        grid=(DEC_BATCH, nj),
        in_specs=[
            pl.BlockSpec((None, N_HEADS, DEC_SEQ, HEAD_DIM), per_b4),
            pl.BlockSpec((None, N_HEADS, HEAD_DIM, PAGE_SIZE), per_b4),
            pl.BlockSpec((None, N_HEADS, HEAD_DIM, PAGE_SIZE), per_b4),
            pl.BlockSpec((None, N_HEADS, PAGE_SIZE), lambda b, j, pt: (b, 0, 0)),
        ] + [lf_spec(i) for i in range(g)] + [page_spec(i) for i in range(g)]
        + [page_spec(i) for i in range(g)],
        out_specs=pl.BlockSpec((None, N_HEADS, DEC_SEQ, HEAD_DIM), per_b4),
        scratch_shapes=[state, state, pltpu.VMEM((N_HEADS, DEC_SEQ, HEAD_DIM), f32), state,
                        pltpu.VMEM((N_HEADS, PAGE_SIZE), f32)],
    )
    return pl.pallas_call(
        _fox_sample_kernel,
        grid_spec=grid_spec,
        out_shape=jax.ShapeDtypeStruct((DEC_BATCH, N_HEADS, DEC_SEQ, HEAD_DIM), f32),
        compiler_params=pltpu.CompilerParams(
            dimension_semantics=("parallel", "arbitrary"), vmem_limit_bytes=VMEM_LIMIT),
        name="fox_sample",
    )(page_table, q4, kn_t, vn_t, lfn_t, *([cache_lft] * g), *([cache_kt] * g), *([cache_vt] * g))


def _oproj_kernel(x_ref, at_ref, as_ref, m_ref, wo_ref, o_ref):
    i = pl.program_id(0)
    wa = wo_ref[0:ATTN_WIDTH, :]
    base = x_ref[...] + _dot(m_ref[...], wo_ref[ATTN_WIDTH:, :])

    @pl.when(i < N_PROMPT_TILES)
    def _():
        o_ref[...] = base + lax.dot_general(at_ref[...], wa, (((0,), (0,)), ((), ())),
                                            preferred_element_type=f32)

    @pl.when(i == N_PROMPT_TILES)
    def _():
        o_ref[...] = base + _dot(as_ref[...], wa)


def _out_proj(x, att_t, att_s, gm, wo):
    row = lambda i: (i, 0)
    fixed = lambda i: (0, 0)
    return pl.pallas_call(
        _oproj_kernel,
        grid=(N_ROWS // PROJ_TM,),
        in_specs=[
            pl.BlockSpec((PROJ_TM, D_MODEL), row),
            pl.BlockSpec((ATTN_WIDTH, PROJ_TM), lambda i: (0, jnp.minimum(i, N_PROMPT_TILES - 1))),
            pl.BlockSpec((N_SAMPLE, ATTN_WIDTH), fixed),
            pl.BlockSpec((PROJ_TM, GMLP_WIDTH), row),
            pl.BlockSpec((ATTN_WIDTH + GMLP_WIDTH, D_MODEL), fixed),
        ],
        out_specs=pl.BlockSpec((PROJ_TM, D_MODEL), row),
        out_shape=jax.ShapeDtypeStruct((N_ROWS, D_MODEL), f32),
        compiler_params=pltpu.CompilerParams(
            dimension_semantics=("arbitrary",), vmem_limit_bytes=VMEM_LIMIT),
        name="out_proj",
    )(x, att_t, att_s, gm, wo)


def kernel(x_prompt, x_sample, cache_k, cache_v, cache_logf, page_table, g_ffn1, w1_gate, w1_up, w1_down, g_mix, w_in, b_f, g_gv, w_s, b_s, w_out, g_ffn2, w2_gate, w2_up, w2_down, g_final):
    l = 0
    x_in = (x_prompt.reshape(N_PROMPT, D_MODEL), x_sample.reshape(N_SAMPLE, D_MODEL))
    g_fin = g_final.reshape(1, D_MODEL)

    w1g, w1u, w1d = w1_gate[l].astype(bf16), w1_up[l].astype(bf16), w1_down[l].astype(bf16)
    w2g, w2u, w2d = w2_gate[l].astype(bf16), w2_up[l].astype(bf16), w2_down[l].astype(bf16)
    wt = w_in[l].T.astype(bf16)
    a = ATTN_WIDTH
    wf = jnp.pad(wt[3 * a:3 * a + N_HEADS], ((0, LANES - N_HEADS), (0, 0)))
    wu = wt[3 * a + N_HEADS:3 * a + N_HEADS + GMLP_WIDTH]
    wgv = wt[3 * a + N_HEADS + GMLP_WIDTH:]
    bfp = jnp.pad(b_f[l].reshape(1, N_HEADS), ((0, 0), (0, LANES - N_HEADS)))
    wo = w_out[l].astype(bf16)

    reps = CHUNK // DEC_SEQ
    ws2 = jnp.stack([w_s[l], jnp.tile(w_s[l][:, :DEC_SEQ, :DEC_SEQ], (1, reps, reps))])
    ti = jnp.arange(CHUNK)
    tril = (ti[None, :] <= ti[:, None])
    same = (ti[None, :] // DEC_SEQ) == (ti[:, None] // DEC_SEQ)
    mask2 = jnp.stack([tril, tril & same]).astype(f32)
    bs2 = jnp.stack([b_s[l].T, jnp.tile(b_s[l][:, :DEC_SEQ], (1, reps)).T])

    x1 = _ffn(x_in, g_ffn1[l].reshape(1, D_MODEL), w1g, w1u, w1d, g_fin, True)

    gm = g_mix[l].reshape(1, D_MODEL)
    qb, kb, kt_p, vt_p, vtb, lf_pad, q_s, kt_s, vt_s = _attn_proj(x1, gm, wt, wf, bfp)
    gmlp_out, gvn_s = _gmlp(x1, gm, wu, wgv, g_gv[l], ws2, mask2, bs2)

    c = _prompt_cumsum(lf_pad)
    att_t = _fox_prompt(qb, kb, vtb, c)

    def sample_keys_last(xt):
        return xt.reshape(N_HEADS, HEAD_DIM, DEC_BATCH, DEC_SEQ).transpose(2, 0, 1, 3)

    def as_page(x4):
        return jnp.pad(x4, ((0, 0), (0, 0), (0, 0), (0, PAGE_SIZE - DEC_SEQ)))

    k_s4, v_s4 = sample_keys_last(kt_s), sample_keys_last(vt_s)
    q4 = q_s.reshape(DEC_BATCH, DEC_SEQ, N_HEADS, HEAD_DIM).transpose(0, 2, 1, 3)
    lf = lf_pad[:, :N_HEADS]
    lfn_t = jnp.pad(lf[N_PROMPT:].reshape(DEC_BATCH, DEC_SEQ, N_HEADS).transpose(0, 2, 1),
                    ((0, 0), (0, 0), (0, PAGE_SIZE - DEC_SEQ)))
    cache_kt = cache_k[l].transpose(0, 2, 3, 1)
    cache_vt = cache_v[l].transpose(0, 2, 3, 1)
    cache_lft = cache_logf[l].transpose(0, 2, 1)
    att_s = _fox_sample(page_table, q4, as_page(k_s4), as_page(v_s4), lfn_t, cache_lft, cache_kt, cache_vt)
    att_s = att_s.transpose(0, 2, 1, 3).reshape(N_SAMPLE, ATTN_WIDTH).astype(bf16)

    x2 = _out_proj(x1, att_t, att_s, gmlp_out, wo)
    y_p, y_s = _ffn((x2,), g_ffn2[l].reshape(1, D_MODEL), w2g, w2u, w2d, g_fin, False)

    def prompt_rows(xt):
        return xt.reshape(BATCH, N_HEADS, HEAD_DIM, SEQ).transpose(0, 3, 1, 2)[None]

    def sample_rows(x4):
        return x4.transpose(0, 3, 1, 2)[None]

    return (y_p.reshape(BATCH, SEQ, D_MODEL),
            y_s.reshape(DEC_BATCH, DEC_SEQ, D_MODEL),
            prompt_rows(kt_p), prompt_rows(vt_p),
            lf[:N_PROMPT].reshape(1, BATCH, SEQ, N_HEADS),
            sample_rows(k_s4), sample_rows(v_s4),
            lf[N_PROMPT:].reshape(1, DEC_BATCH, DEC_SEQ, N_HEADS),
            gvn_s.reshape(1, DEC_BATCH, DEC_SEQ, N_GMLP_HEADS, GMLP_HEAD_DIM))
```

```python
import functools

import jax
import jax.numpy as jnp
from jax import lax
from jax.experimental import pallas as pl
from jax.experimental.pallas import tpu as pltpu

f32 = jnp.float32
bf16 = jnp.bfloat16

D_MODEL = 2048
BATCH = 2
SEQ = 4096
DEC_BATCH = 32
DEC_SEQ = 8
PAST_LEN = 16384
PAGE_SIZE = 128
ATTN_WIDTH = 1024
GMLP_WIDTH = 1024
HEAD_DIM = 64
N_HEADS = 16
CHUNK = 128
GMLP_HEAD_DIM = 128
N_GMLP_HEADS = 8
D_FF = 5504
EPS = 1e-6
NEG_INF = -1e30
N_PAGES = PAST_LEN // PAGE_SIZE

LANES = 128
SUBLANES = 8
MXU_DIM = 256

N_PROMPT = BATCH * SEQ
N_SAMPLE = DEC_BATCH * DEC_SEQ
N_ROWS = N_PROMPT + N_SAMPLE

FFN_TM = 768
FFN_TF = 512
FFN_NF = -(-D_FF // FFN_TF)
FFN_TAIL = D_FF - (FFN_NF - 1) * FFN_TF
FFN_PROMPT_TAIL = N_PROMPT % FFN_TM
assert FFN_PROMPT_TAIL + N_SAMPLE == FFN_TM and FFN_PROMPT_TAIL % SUBLANES == 0
PROJ_TM = 256
ATT_T = 512
ATT_SUM_ROWS = 2 * SUBLANES
LOG2E = 1.4426950408889634
PAGES_PER_STEP = 16
VMEM_LIMIT = 56 * 1024 * 1024


def _rms(x, g):
    return x * lax.rsqrt(jnp.mean(x * x, axis=-1, keepdims=True) + EPS) * g


def _split3(x):
    hi = x.astype(bf16)
    r1 = x - hi.astype(f32)
    mid = r1.astype(bf16)
    lo = (r1 - mid.astype(f32)).astype(bf16)
    return hi, mid, lo


def _dot(a, b):
    return jnp.dot(a, b, preferred_element_type=f32)


def _dot_nt(a, b):
    return lax.dot_general(a, b, (((1,), (1,)), ((), ())), preferred_element_type=f32)


def _dot3(x, sel):
    hi, mid, lo = _split3(x)
    return _dot(hi, sel) + _dot(mid, sel) + _dot(lo, sel)


def _ffn_kernel(*refs, first, nf):
    if first:
        x_ref, xs_ref, g_ref, wg_ref, wu_ref, wd_ref, gfin_ref, o_ref, h_ref = refs
    else:
        x_ref, g_ref, wg_ref, wu_ref, wd_ref, gfin_ref, o_ref, os_ref, h_ref = refs
    i = pl.program_id(0)
    f = pl.program_id(1)
    last_tile = pl.num_programs(0) - 1
    final_norm = not first

    def start(x):
        h_ref[...] = _rms(x, g_ref[...]).astype(bf16)
        o_ref[...] = x

    if first:
        @pl.when((f == 0) & (i < last_tile))
        def _():
            start(x_ref[...])

        @pl.when((f == 0) & (i == last_tile))
        def _():
            start(jnp.concatenate([x_ref[0:FFN_PROMPT_TAIL, :], xs_ref[...]], axis=0))
    else:
        @pl.when(f == 0)
        def _():
            start(x_ref[...])

    def step(tail):
        h = h_ref[...]
        gate = _dot(h, wg_ref[...])
        up = _dot(h, wu_ref[...])
        act = (gate / (1.0 + jnp.exp(-gate))) * up * 0.5
        wd = wd_ref[...]
        if tail:
            col = lax.broadcasted_iota(jnp.int32, act.shape, 1)
            act = jnp.where(col < FFN_TAIL, act, 0.0)
            row = lax.broadcasted_iota(jnp.int32, wd.shape, 0)
            wd = jnp.where(row < FFN_TAIL, wd, 0.0)
        o_ref[...] += _dot(act.astype(bf16), wd.astype(bf16))

    @pl.when(f < nf - 1)
    def _():
        step(False)

    @pl.when(f == nf - 1)
    def _():
        step(True)
        if final_norm:
            o_ref[...] = _rms(o_ref[...], gfin_ref[...])

            @pl.when(i == last_tile)
            def _():
                os_ref[...] = o_ref[FFN_PROMPT_TAIL:, :]


def _ffn(xs, g, wg, wu, wd, gfin, first):
    nf = FFN_NF
    tile = pl.BlockSpec((FFN_TM, D_MODEL), lambda i, f: (i, 0))
    sample = pl.BlockSpec((N_SAMPLE, D_MODEL), lambda i, f: (0, 0))
    vec = pl.BlockSpec((1, D_MODEL), lambda i, f: (0, 0))
    weights = [pl.BlockSpec((D_MODEL, FFN_TF), lambda i, f: (0, f)),
               pl.BlockSpec((D_MODEL, FFN_TF), lambda i, f: (0, f)),
               pl.BlockSpec((FFN_TF, D_MODEL), lambda i, f: (f, 0))]
    if first:
        in_specs = [tile, sample, vec] + weights + [vec]
        out_specs = tile
        out_shape = jax.ShapeDtypeStruct((N_ROWS, D_MODEL), f32)
    else:
        in_specs = [tile, vec] + weights + [vec]
        out_specs = [tile, sample]
        out_shape = [jax.ShapeDtypeStruct((N_PROMPT, D_MODEL), f32),
                     jax.ShapeDtypeStruct((N_SAMPLE, D_MODEL), f32)]
    return pl.pallas_call(
        functools.partial(_ffn_kernel, first=first, nf=nf),
        grid=(N_ROWS // FFN_TM, nf),
        in_specs=in_specs,
        out_specs=out_specs,
        out_shape=out_shape,
        scratch_shapes=[pltpu.VMEM((FFN_TM, D_MODEL), bf16)],
        compiler_params=pltpu.CompilerParams(
            dimension_semantics=("arbitrary", "arbitrary"), vmem_limit_bytes=VMEM_LIMIT),
        name="ffn_half",
    )(*xs, g, wg, wu, wd, gfin)


N_PROMPT_TILES = N_PROMPT // PROJ_TM
TILES_PER_SEQ = SEQ // PROJ_TM
assert N_SAMPLE == PROJ_TM


def _aproj_kernel(x_ref, g_ref, wq_ref, wk_ref, wv_ref, wf_ref, bf_ref,
                  qb_ref, kb_ref, kt_ref, vt_ref, vtb_ref, lf_ref, qs_ref, kts_ref, vts_ref):
    i = pl.program_id(0)
    h = _rms(x_ref[...], g_ref[...]).astype(bf16)
    q = _dot_nt(h, wq_ref[...]) * (HEAD_DIM ** -0.5)
    qb_ref[...] = (q * LOG2E).astype(bf16)
    kb_ref[...] = _dot_nt(h, wk_ref[...]).astype(bf16)
    kt = _dot_nt(wk_ref[...], h)
    vt = _dot_nt(wv_ref[...], h)
    z = _dot_nt(h, wf_ref[...]) + bf_ref[...]
    lf_ref[...] = -(jnp.maximum(-z, 0.0) + jnp.log1p(jnp.exp(-jnp.abs(z))))

    @pl.when(i < N_PROMPT_TILES)
    def _():
        kt_ref[...] = kt
        vt_ref[...] = vt
        vtb_ref[...] = vt.astype(bf16)

    @pl.when(i == N_PROMPT_TILES)
    def _():
        qs_ref[...] = q
        kts_ref[...] = kt
        vts_ref[...] = vt


def _attn_proj(x, g, wt, wf, bfp):
    row = lambda i: (i, 0)
    fixed = lambda i: (0, 0)

    def seq_tile(i):
        t = jnp.minimum(i, N_PROMPT_TILES - 1)
        return (t // TILES_PER_SEQ, 0, t % TILES_PER_SEQ)

    wide = lambda dt: jax.ShapeDtypeStruct((N_ROWS, ATTN_WIDTH), dt)
    tposed = lambda dt: jax.ShapeDtypeStruct((BATCH, ATTN_WIDTH, SEQ), dt)
    small_t = jax.ShapeDtypeStruct((ATTN_WIDTH, N_SAMPLE), f32)
    w_block = lambda n: pl.BlockSpec((ATTN_WIDTH, D_MODEL), lambda i: (n, 0))
    return pl.pallas_call(
        _aproj_kernel,
        grid=(N_ROWS // PROJ_TM,),
        in_specs=[
            pl.BlockSpec((PROJ_TM, D_MODEL), row),
            pl.BlockSpec((1, D_MODEL), fixed),
            w_block(0), w_block(1), w_block(2),
            pl.BlockSpec((LANES, D_MODEL), fixed),
            pl.BlockSpec((1, LANES), fixed),
        ],
        out_specs=[pl.BlockSpec((PROJ_TM, ATTN_WIDTH), row)] * 2
        + [pl.BlockSpec((None, ATTN_WIDTH, PROJ_TM), seq_tile)] * 3
        + [pl.BlockSpec((PROJ_TM, LANES), row),
           pl.BlockSpec((N_SAMPLE, ATTN_WIDTH), fixed),
           pl.BlockSpec((ATTN_WIDTH, N_SAMPLE), fixed),
           pl.BlockSpec((ATTN_WIDTH, N_SAMPLE), fixed)],
        out_shape=[wide(bf16), wide(bf16), tposed(f32), tposed(f32), tposed(bf16),
                   jax.ShapeDtypeStruct((N_ROWS, LANES), f32),
                   jax.ShapeDtypeStruct((N_SAMPLE, ATTN_WIDTH), f32), small_t, small_t],
        compiler_params=pltpu.CompilerParams(
            dimension_semantics=("arbitrary",), vmem_limit_bytes=VMEM_LIMIT),
        name="attn_proj",
    )(x, g, wt, wt, wt, wf, bfp)


def _gelu_tanh(x):
    return 0.5 * x * (1.0 + jnp.tanh(0.7978845608028654 * (x + 0.044715 * (x * x * x))))


def _gmlp_kernel(x_ref, g_ref, wu_ref, wv_ref, ggv_ref, ws_ref, mask_ref, bs_ref,
                 o_ref, gvn_ref, *, n_tiles):
    i = pl.program_id(0)
    h = _rms(x_ref[...], g_ref[...]).astype(bf16)
    mask = mask_ref[...]
    bias = bs_ref[...]
    pair_w = 2 * GMLP_HEAD_DIM

    def project(gp):
        rows = slice(gp * pair_w, (gp + 1) * pair_w)
        return _dot_nt(h, wu_ref[rows, :]), _dot_nt(h, wv_ref[rows, :])

    def finish(gp, u_lin, gv_lin):
        u = _gelu_tanh(u_lin)
        gv = _gelu_tanh(gv_lin)
        for e in range(2):
            g = 2 * gp + e
            loc = slice(e * GMLP_HEAD_DIM, (e + 1) * GMLP_HEAD_DIM)
            cols = slice(g * GMLP_HEAD_DIM, (g + 1) * GMLP_HEAD_DIM)
            gvn = _rms(gv[:, loc], ggv_ref[g:g + 1, :])
            w = (ws_ref[g] * mask).astype(bf16)
            for c in range(PROJ_TM // CHUNK):
                rows = slice(c * CHUNK, (c + 1) * CHUNK)
                mixed = _dot(w, gvn[rows].astype(bf16)) + bias[:, g:g + 1]
                o_ref[rows, cols] = (u[rows, loc] * mixed).astype(bf16)

            @pl.when(i == n_tiles - 1)
            def _():
                gvn_ref[:, cols] = gvn

    n_pairs = N_GMLP_HEADS // 2
    cur = project(0)
    for gp in range(n_pairs):
        nxt = project(gp + 1) if gp + 1 < n_pairs else None
        finish(gp, *cur)
        cur = nxt


def _gmlp(x, g, wu, wv, ggv, ws2, mask2, bs2):
    n_tiles = N_ROWS // PROJ_TM
    n_prompt_tiles = N_PROMPT // PROJ_TM
    row = lambda i: (i, 0)
    fixed = lambda i: (0, 0)
    kind3 = lambda i: (i // n_prompt_tiles, 0, 0)
    kind4 = lambda i: (i // n_prompt_tiles, 0, 0, 0)
    return pl.pallas_call(
        functools.partial(_gmlp_kernel, n_tiles=n_tiles),
        grid=(n_tiles,),
        in_specs=[
            pl.BlockSpec((PROJ_TM, D_MODEL), row),
            pl.BlockSpec((1, D_MODEL), fixed),
            pl.BlockSpec((GMLP_WIDTH, D_MODEL), fixed),
            pl.BlockSpec((GMLP_WIDTH, D_MODEL), fixed),
            pl.BlockSpec((N_GMLP_HEADS, GMLP_HEAD_DIM), fixed),
            pl.BlockSpec((None, N_GMLP_HEADS, CHUNK, CHUNK), kind4),
            pl.BlockSpec((None, CHUNK, CHUNK), kind3),
            pl.BlockSpec((None, CHUNK, N_GMLP_HEADS), kind3),
        ],
        out_specs=[pl.BlockSpec((PROJ_TM, GMLP_WIDTH), row),
                   pl.BlockSpec((PROJ_TM, GMLP_WIDTH), fixed)],
        out_shape=[jax.ShapeDtypeStruct((N_ROWS, GMLP_WIDTH), bf16),
                   jax.ShapeDtypeStruct((N_SAMPLE, GMLP_WIDTH), f32)],
        compiler_params=pltpu.CompilerParams(
            dimension_semantics=("arbitrary",), vmem_limit_bytes=VMEM_LIMIT),
        name="gmlp_group",
    )(x, g, wu, wv, ggv, ws2, mask2, bs2)


def _cumsum_kernel(lf_ref, c_ref):
    r = lax.broadcasted_iota(jnp.int32, (CHUNK, CHUNK), 0)
    s = lax.broadcasted_iota(jnp.int32, (CHUNK, CHUNK), 1)
    tri = (s <= r).astype(bf16)

    def body(j, carry):
        rows = pl.ds(pl.multiple_of(j * CHUNK, CHUNK), CHUNK)
        hi, mid, lo = _split3(lf_ref[rows, :])
        c = _dot(tri, hi) + _dot(tri, mid) + _dot(tri, lo) + carry
        c_ref[rows, :] = c
        return c[CHUNK - 1:CHUNK, :]

    lax.fori_loop(0, SEQ // CHUNK, body, jnp.zeros((1, LANES), f32))


def _prompt_cumsum(lf_pad):
    return pl.pallas_call(
        _cumsum_kernel,
        grid=(BATCH,),
        in_specs=[pl.BlockSpec((SEQ, LANES), lambda b: (b, 0))],
        out_specs=pl.BlockSpec((SEQ, LANES), lambda b: (b, 0)),
        out_shape=jax.ShapeDtypeStruct((N_PROMPT, LANES), f32),
        compiler_params=pltpu.CompilerParams(
            dimension_semantics=("parallel",), vmem_limit_bytes=VMEM_LIMIT),
        name="logf_cumsum",
    )(lf_pad)


N_SPLIT = 3


def _bias_lanes(e):
    return HEAD_DIM * (1 - e)


def _fox_prompt_kernel(q_ref, k_ref, vt_ref, c_ref, o_ref,
                       kp_ref, g_ref, qp_ref, sa_ref, sb_ref, mxa_ref, mxb_ref, m_ref, acc_ref):
    hp = pl.program_id(1)
    qi = pl.program_id(2)
    t = ATT_T
    lane = lax.broadcasted_iota(jnp.int32, (t, LANES), 1)

    def in_span(e, first, n):
        base = _bias_lanes(e) + first
        return (lane >= base) & (lane < base + n)

    @pl.when(qi == 0)
    def _():
        r = lax.broadcasted_iota(jnp.int32, (LANES, LANES), 0)
        l2 = lax.broadcasted_iota(jnp.int32, (LANES, LANES), 1)
        sels = []
        for j in range(N_SPLIT):
            sel = jnp.zeros((LANES, LANES), f32)
            for e in range(2):
                src = r == 2 * hp + e
                sel = sel - (src & (l2 == _bias_lanes(e) + j)).astype(f32)
                sel = sel + (src & (l2 == _bias_lanes(e) + N_SPLIT + j)).astype(f32)
            sels.append(sel.astype(bf16))

        def prep(i, _):
            rows = pl.ds(pl.multiple_of(i * t, t), t)
            parts = _split3(c_ref[rows, :] * LOG2E)
            g = sum(_dot(p, s) for p, s in zip(parts, sels))
            g_ref[rows, :] = g
            k2 = k_ref[rows, :].astype(f32)
            for e in range(2):
                own = (lane < HEAD_DIM) if e == 0 else (lane >= HEAD_DIM)
                extra = jnp.where(in_span(e, 0, N_SPLIT), g,
                                  jnp.where(in_span(e, N_SPLIT, N_SPLIT), 1.0, 0.0))
                kp_ref[e, rows, :] = jnp.where(own, k2, extra).astype(bf16)
            return 0

        lax.fori_loop(0, SEQ // t, prep, 0)

    q0 = pl.multiple_of(qi * t, t)
    q2 = q_ref[...].astype(f32)
    gq = g_ref[pl.ds(q0, t), :]
    for e in range(2):
        own = (lane < HEAD_DIM) if e == 0 else (lane >= HEAD_DIM)
        extra = jnp.where(in_span(e, 0, N_SPLIT), 1.0,
                          jnp.where(in_span(e, N_SPLIT, N_SPLIT), gq, 0.0))
        qp_ref[e] = jnp.where(own, q2, extra).astype(bf16)

    m_ref[...] = jnp.full(m_ref.shape, NEG_INF, f32)
    acc_ref[...] = jnp.zeros(acc_ref.shape, f32)
    ones = jnp.ones((ATT_SUM_ROWS, t), bf16)

    def scores(blk, s_ref, mx_ref):
        k0 = pl.multiple_of(blk * t, t)
        for e in range(2):
            st = _dot_nt(kp_ref[e, pl.ds(k0, t), :], qp_ref[e])
            s_ref[e] = st
            mx_ref[e] = jnp.max(st, axis=0, keepdims=True)

    def consume(blk, s_ref, mx_ref, on_diagonal):
        k0 = pl.multiple_of(blk * t, t)
        for e in range(2):
            st = s_ref[e]
            if on_diagonal:
                key = lax.broadcasted_iota(jnp.int32, (t, t), 0)
                qry = lax.broadcasted_iota(jnp.int32, (t, t), 1)
                st = jnp.where(key <= qry, st, NEG_INF)
                mx = jnp.max(st, axis=0, keepdims=True)
            else:
                mx = mx_ref[e]
            m_old = m_ref[e]
            m_new = jnp.maximum(m_old, mx)
            alpha = jnp.exp2(m_old - m_new)
            pt = jnp.exp2(st - m_new)
            m_ref[e] = m_new
            vt = jnp.concatenate([vt_ref[e * HEAD_DIM:(e + 1) * HEAD_DIM, pl.ds(k0, t)], ones], axis=0)
            acc_ref[e] = acc_ref[e] * alpha + _dot(vt, pt.astype(bf16))

    scores(0, sa_ref, mxa_ref)

    def pair(p, _):
        j = 2 * p
        scores(j + 1, sb_ref, mxb_ref)
        consume(j, sa_ref, mxa_ref, False)
        scores(j + 2, sa_ref, mxa_ref)
        consume(j + 1, sb_ref, mxb_ref, False)
        return 0

    lax.fori_loop(0, qi // 2, pair, 0)

    @pl.when(qi % 2 == 0)
    def _():
        consume(qi, sa_ref, mxa_ref, True)

    @pl.when(qi % 2 == 1)
    def _():
        scores(qi, sb_ref, mxb_ref)
        consume(qi - 1, sa_ref, mxa_ref, False)
        consume(qi, sb_ref, mxb_ref, True)

    for e in range(2):
        acc = acc_ref[e]
        o_ref[e * HEAD_DIM:(e + 1) * HEAD_DIM, :] = (
            acc[0:HEAD_DIM] / acc[HEAD_DIM:HEAD_DIM + 1]).astype(bf16)


def _fox_prompt(qb, kb, vt, c):
    nq = SEQ // ATT_T
    return pl.pallas_call(
        _fox_prompt_kernel,
        grid=(BATCH, N_HEADS // 2, nq),
        in_specs=[
            pl.BlockSpec((ATT_T, LANES), lambda b, hp, qi: (b * nq + qi, hp)),
            pl.BlockSpec((SEQ, LANES), lambda b, hp, qi: (b, hp)),
            pl.BlockSpec((None, LANES, SEQ), lambda b, hp, qi: (b, hp, 0)),
            pl.BlockSpec((SEQ, LANES), lambda b, hp, qi: (b, 0)),
        ],
        out_specs=pl.BlockSpec((LANES, ATT_T), lambda b, hp, qi: (hp, b * nq + qi)),
        out_shape=jax.ShapeDtypeStruct((ATTN_WIDTH, N_PROMPT), bf16),
        scratch_shapes=[pltpu.VMEM((2, SEQ, LANES), bf16), pltpu.VMEM((SEQ, LANES), f32),
                        pltpu.VMEM((2, ATT_T, LANES), bf16),
                        pltpu.VMEM((2, ATT_T, ATT_T), f32), pltpu.VMEM((2, ATT_T, ATT_T), f32),
                        pltpu.VMEM((2, 1, ATT_T), f32), pltpu.VMEM((2, 1, ATT_T), f32),
                        pltpu.VMEM((2, 1, ATT_T), f32),
                        pltpu.VMEM((2, HEAD_DIM + ATT_SUM_ROWS, ATT_T), f32)],
        compiler_params=pltpu.CompilerParams(
            dimension_semantics=("parallel", "parallel", "arbitrary"),
            vmem_limit_bytes=VMEM_LIMIT),
        name="fox_prompt",
    )(qb, kb, vt, c)


def _fox_sample_kernel(pt_ref, q_ref, kn_ref, vn_ref, lfn_ref, *refs):
    g = PAGES_PER_STEP
    lf_refs = refs[:g]
    k_refs = refs[g:2 * g]
    v_refs = refs[2 * g:3 * g]
    o_ref, m_ref, l_ref, acc_ref, cn_ref, carry_ref = refs[3 * g:]
    j = pl.program_id(1)
    tok = lax.broadcasted_iota(jnp.int32, (DEC_SEQ, PAGE_SIZE), 0)
    key = lax.broadcasted_iota(jnp.int32, (DEC_SEQ, PAGE_SIZE), 1)
    j_i = lax.broadcasted_iota(jnp.int32, (PAGE_SIZE, PAGE_SIZE), 0)
    s_i = lax.broadcasted_iota(jnp.int32, (PAGE_SIZE, PAGE_SIZE), 1)

    @pl.when(j == 0)
    def _():
        carry_ref[...] = jnp.zeros(carry_ref.shape, f32)
        upto = (j_i <= s_i).astype(bf16)
        c_new = _dot3(lfn_ref[...], upto)
        own = []
        for h in range(N_HEADS):
            c_row = c_new[h:h + 1, :]
            c_col = jnp.sum(jnp.where(key == tok, c_row, 0.0), axis=1, keepdims=True)
            c_b = jnp.broadcast_to(c_col, (DEC_SEQ, PAGE_SIZE))
            cn_ref[h] = c_b
            s = _dot(q_ref[h], kn_ref[h]) + (c_b - c_row)
            own.append(jnp.where(key <= tok, s, NEG_INF))
        for h in range(N_HEADS):
            s = own[h]
            m = jnp.max(s, axis=1, keepdims=True)
            p = jnp.exp(s - m)
            m_ref[h] = jnp.broadcast_to(m, (DEC_SEQ, PAGE_SIZE))
            l_ref[h] = p
            acc_ref[h] = _dot_nt(p, vn_ref[h])

    after = (j_i > s_i).astype(bf16)
    x = jnp.concatenate([lf_refs[pg][...] for pg in range(g)], axis=0)
    within = _dot3(x, after)
    total = jnp.sum(x, axis=1, keepdims=True)
    carry = carry_ref[...]
    sfx = []
    for pg in range(g):
        rows = slice(pg * N_HEADS, (pg + 1) * N_HEADS)
        sfx.append(within[rows] + carry)
        carry = carry + total[rows]
    carry_ref[...] = carry

    scores, row_max = [], []
    for h in range(N_HEADS):
        q = q_ref[h]
        bias = cn_ref[h]
        s = [_dot(q, k_refs[pg][h]) + (bias + sfx[pg][h:h + 1, :]) for pg in range(g)]
        mx = s[0]
        for pg in range(1, g):
            mx = jnp.maximum(mx, s[pg])
        scores.append(s)
        row_max.append(jnp.max(mx, axis=1, keepdims=True))
    for h in range(N_HEADS):
        s = scores[h]
        m_old = m_ref[h]
        m_new = jnp.maximum(m_old, row_max[h])
        alpha = jnp.exp(m_old - m_new)
        p = [jnp.exp(s[pg] - m_new) for pg in range(g)]
        l_ref[h] = alpha * l_ref[h] + sum(p)
        m_ref[h] = m_new
        pv = sum(_dot_nt(p[pg], v_refs[pg][h]) for pg in range(g))
        acc_ref[h] = acc_ref[h] * alpha[:, :HEAD_DIM] + pv

    @pl.when(j == pl.num_programs(1) - 1)
    def _():
        for h in range(N_HEADS):
            o_ref[h] = acc_ref[h] / jnp.sum(l_ref[h], axis=1, keepdims=True)


def _fox_sample(page_table, q4, kn_t, vn_t, lfn_t, cache_lft, cache_kt, cache_vt):
    g = PAGES_PER_STEP
    nj = N_PAGES // g

    def page_of(b, j, pt, i):
        return pt[b, N_PAGES - 1 - (j * g + i)]

    per_b4 = lambda b, j, pt: (b, 0, 0, 0)
    page_spec = lambda i: pl.BlockSpec((None, N_HEADS, HEAD_DIM, PAGE_SIZE),
                                       lambda b, j, pt: (page_of(b, j, pt, i), 0, 0, 0))
    lf_spec = lambda i: pl.BlockSpec((None, N_HEADS, PAGE_SIZE),
                                     lambda b, j, pt: (page_of(b, j, pt, i), 0, 0))
    state = pltpu.VMEM((N_HEADS, DEC_SEQ, PAGE_SIZE), f32)
    grid_spec = pltpu.PrefetchScalarGridSpec(
        num_scalar_prefetch=1,
        grid=(DEC_BATCH, nj),
        in_specs=[
            pl.BlockSpec((None, N_HEADS, DEC_SEQ, HEAD_DIM), per_b4),
            pl.BlockSpec((None, N_HEADS, HEAD_DIM, PAGE_SIZE), per_b4),
            pl.BlockSpec((None, N_HEADS, HEAD_DIM, PAGE_SIZE), per_b4),
            pl.BlockSpec((None, N_HEADS, PAGE_SIZE), lambda b, j, pt: (b, 0, 0)),
        ] + [lf_spec(i) for i in range(g)] + [page_spec(i) for i in range(g)]
        + [page_spec(i) for i in range(g)],
        out_specs=pl.BlockSpec((None, N_HEADS, DEC_SEQ, HEAD_DIM), per_b4),
        scratch_shapes=[state, state, pltpu.VMEM((N_HEADS, DEC_SEQ, HEAD_DIM), f32), state,
                        pltpu.VMEM((N_HEADS, PAGE_SIZE), f32)],
    )
    return pl.pallas_call(
        _fox_sample_kernel,
        grid_spec=grid_spec,
        out_shape=jax.ShapeDtypeStruct((DEC_BATCH, N_HEADS, DEC_SEQ, HEAD_DIM), f32),
        compiler_params=pltpu.CompilerParams(
            dimension_semantics=("parallel", "arbitrary"), vmem_limit_bytes=VMEM_LIMIT),
        name="fox_sample",
    )(page_table, q4, kn_t, vn_t, lfn_t, *([cache_lft] * g), *([cache_kt] * g), *([cache_vt] * g))


def _oproj_kernel(x_ref, at_ref, as_ref, m_ref, wo_ref, o_ref):
    i = pl.program_id(0)
    wa = wo_ref[0:ATTN_WIDTH, :]
    base = x_ref[...] + _dot(m_ref[...], wo_ref[ATTN_WIDTH:, :])

    @pl.when(i < N_PROMPT_TILES)
    def _():
        o_ref[...] = base + lax.dot_general(at_ref[...], wa, (((0,), (0,)), ((), ())),
                                            preferred_element_type=f32)

    @pl.when(i == N_PROMPT_TILES)
    def _():
        o_ref[...] = base + _dot(as_ref[...], wa)


def _out_proj(x, att_t, att_s, gm, wo):
    row = lambda i: (i, 0)
    fixed = lambda i: (0, 0)
    return pl.pallas_call(
        _oproj_kernel,
        grid=(N_ROWS // PROJ_TM,),
        in_specs=[
            pl.BlockSpec((PROJ_TM, D_MODEL), row),
            pl.BlockSpec((ATTN_WIDTH, PROJ_TM), lambda i: (0, jnp.minimum(i, N_PROMPT_TILES - 1))),
            pl.BlockSpec((N_SAMPLE, ATTN_WIDTH), fixed),
            pl.BlockSpec((PROJ_TM, GMLP_WIDTH), row),
            pl.BlockSpec((ATTN_WIDTH + GMLP_WIDTH, D_MODEL), fixed),
        ],
        out_specs=pl.BlockSpec((PROJ_TM, D_MODEL), row),
        out_shape=jax.ShapeDtypeStruct((N_ROWS, D_MODEL), f32),
        compiler_params=pltpu.CompilerParams(
            dimension_semantics=("arbitrary",), vmem_limit_bytes=VMEM_LIMIT),
        name="out_proj",
    )(x, att_t, att_s, gm, wo)


def kernel(x_prompt, x_sample, cache_k, cache_v, cache_logf, page_table, g_ffn1, w1_gate, w1_up, w1_down, g_mix, w_in, b_f, g_gv, w_s, b_s, w_out, g_ffn2, w2_gate, w2_up, w2_down, g_final):
    l = 0
    x_in = (x_prompt.reshape(N_PROMPT, D_MODEL), x_sample.reshape(N_SAMPLE, D_MODEL))
    g_fin = g_final.reshape(1, D_MODEL)

    w1g, w1u, w1d = w1_gate[l].astype(bf16), w1_up[l].astype(bf16), w1_down[l]
    w2g, w2u, w2d = w2_gate[l].astype(bf16), w2_up[l].astype(bf16), w2_down[l]
    wt = w_in[l].T.astype(bf16)
    a = ATTN_WIDTH
    wf = jnp.pad(wt[3 * a:3 * a + N_HEADS], ((0, LANES - N_HEADS), (0, 0)))
    wu = wt[3 * a + N_HEADS:3 * a + N_HEADS + GMLP_WIDTH]
    wgv = wt[3 * a + N_HEADS + GMLP_WIDTH:]
    bfp = jnp.pad(b_f[l].reshape(1, N_HEADS), ((0, 0), (0, LANES - N_HEADS)))
    wo = w_out[l].astype(bf16)

    reps = CHUNK // DEC_SEQ
    ws2 = jnp.stack([w_s[l], jnp.tile(w_s[l][:, :DEC_SEQ, :DEC_SEQ], (1, reps, reps))])
    ti = jnp.arange(CHUNK)
    tril = (ti[None, :] <= ti[:, None])
    same = (ti[None, :] // DEC_SEQ) == (ti[:, None] // DEC_SEQ)
    mask2 = jnp.stack([tril, tril & same]).astype(f32)
    bs2 = jnp.stack([b_s[l].T, jnp.tile(b_s[l][:, :DEC_SEQ], (1, reps)).T])

    x1 = _ffn(x_in, g_ffn1[l].reshape(1, D_MODEL), w1g, w1u, w1d, g_fin, True)

    gm = g_mix[l].reshape(1, D_MODEL)
    qb, kb, kt_p, vt_p, vtb, lf_pad, q_s, kt_s, vt_s = _attn_proj(x1, gm, wt, wf, bfp)
    gmlp_out, gvn_s = _gmlp(x1, gm, wu, wgv, g_gv[l], ws2, mask2, bs2)

    c = _prompt_cumsum(lf_pad)
    att_t = _fox_prompt(qb, kb, vtb, c)

    def sample_keys_last(xt):
        return xt.reshape(N_HEADS, HEAD_DIM, DEC_BATCH, DEC_SEQ).transpose(2, 0, 1, 3)

    def as_page(x4):
        return jnp.pad(x4, ((0, 0), (0, 0), (0, 0), (0, PAGE_SIZE - DEC_SEQ)))

    k_s4, v_s4 = sample_keys_last(kt_s), sample_keys_last(vt_s)
    q4 = q_s.reshape(DEC_BATCH, DEC_SEQ, N_HEADS, HEAD_DIM).transpose(0, 2, 1, 3)
    lf = lf_pad[:, :N_HEADS]
    lfn_t = jnp.pad(lf[N_PROMPT:].reshape(DEC_BATCH, DEC_SEQ, N_HEADS).transpose(0, 2, 1),
                    ((0, 0), (0, 0), (0, PAGE_SIZE - DEC_SEQ)))
    cache_kt = cache_k[l].transpose(0, 2, 3, 1)
    cache_vt = cache_v[l].transpose(0, 2, 3, 1)
    cache_lft = cache_logf[l].transpose(0, 2, 1)
    att_s = _fox_sample(page_table, q4, as_page(k_s4), as_page(v_s4), lfn_t, cache_lft, cache_kt, cache_vt)
    att_s = att_s.transpose(0, 2, 1, 3).reshape(N_SAMPLE, ATTN_WIDTH).astype(bf16)

    x2 = _out_proj(x1, att_t, att_s, gmlp_out, wo)
    y_p, y_s = _ffn((x2,), g_ffn2[l].reshape(1, D_MODEL), w2g, w2u, w2d, g_fin, False)

    def prompt_rows(xt):
        return xt.reshape(BATCH, N_HEADS, HEAD_DIM, SEQ).transpose(0, 3, 1, 2)[None]

    def sample_rows(x4):
        return x4.transpose(0, 3, 1, 2)[None]

    return (y_p.reshape(BATCH, SEQ, D_MODEL),
            y_s.reshape(DEC_BATCH, DEC_SEQ, D_MODEL),
            prompt_rows(kt_p), prompt_rows(vt_p),
            lf[:N_PROMPT].reshape(1, BATCH, SEQ, N_HEADS),
            sample_rows(k_s4), sample_rows(v_s4),
            lf[N_PROMPT:].reshape(1, DEC_BATCH, DEC_SEQ, N_HEADS),
            gvn_s.reshape(1, DEC_BATCH, DEC_SEQ, N_GMLP_HEADS, GMLP_HEAD_DIM))
```

```python
import functools

import jax
import jax.numpy as jnp
from jax import lax
from jax.experimental import pallas as pl
from jax.experimental.pallas import tpu as pltpu

f32 = jnp.float32
bf16 = jnp.bfloat16

D_MODEL = 2048
BATCH = 2
SEQ = 4096
DEC_BATCH = 32
DEC_SEQ = 8
PAST_LEN = 16384
PAGE_SIZE = 128
ATTN_WIDTH = 1024
GMLP_WIDTH = 1024
HEAD_DIM = 64
N_HEADS = 16
CHUNK = 128
GMLP_HEAD_DIM = 128
N_GMLP_HEADS = 8
D_FF = 5504
EPS = 1e-6
NEG_INF = -1e30
N_PAGES = PAST_LEN // PAGE_SIZE

LANES = 128
SUBLANES = 8
MXU_DIM = 256

N_PROMPT = BATCH * SEQ
N_SAMPLE = DEC_BATCH * DEC_SEQ
N_ROWS = N_PROMPT + N_SAMPLE

FFN_TM = 768
FFN_TF = 512
FFN_NF = -(-D_FF // FFN_TF)
FFN_TAIL = D_FF - (FFN_NF - 1) * FFN_TF
FFN_PROMPT_TAIL = N_PROMPT % FFN_TM
assert FFN_PROMPT_TAIL + N_SAMPLE == FFN_TM and FFN_PROMPT_TAIL % SUBLANES == 0
PROJ_TM = 256
ATT_T = 512
ATT_SUM_ROWS = 2 * SUBLANES
LOG2E = 1.4426950408889634
PAGES_PER_STEP = 16
VMEM_LIMIT = 56 * 1024 * 1024


def _rms(x, g):
    return x * lax.rsqrt(jnp.mean(x * x, axis=-1, keepdims=True) + EPS) * g


def _split3(x):
    hi = x.astype(bf16)
    r1 = x - hi.astype(f32)
    mid = r1.astype(bf16)
    lo = (r1 - mid.astype(f32)).astype(bf16)
    return hi, mid, lo


def _dot(a, b):
    return jnp.dot(a, b, preferred_element_type=f32)


def _dot_nt(a, b):
    return lax.dot_general(a, b, (((1,), (1,)), ((), ())), preferred_element_type=f32)


def _dot3(x, sel):
    hi, mid, lo = _split3(x)
    return _dot(hi, sel) + _dot(mid, sel) + _dot(lo, sel)


def _ffn_kernel(*refs, first, nf):
    if first:
        x_ref, xs_ref, g_ref, wg_ref, wu_ref, wd_ref, gfin_ref, o_ref, h_ref = refs
    else:
        x_ref, g_ref, wg_ref, wu_ref, wd_ref, gfin_ref, o_ref, os_ref, h_ref = refs
    i = pl.program_id(0)
    f = pl.program_id(1)
    last_tile = pl.num_programs(0) - 1
    final_norm = not first

    def start(x):
        h_ref[...] = _rms(x, g_ref[...]).astype(bf16)
        o_ref[...] = x

    if first:
        @pl.when((f == 0) & (i < last_tile))
        def _():
            start(x_ref[...])

        @pl.when((f == 0) & (i == last_tile))
        def _():
            start(jnp.concatenate([x_ref[0:FFN_PROMPT_TAIL, :], xs_ref[...]], axis=0))
    else:
        @pl.when(f == 0)
        def _():
            start(x_ref[...])

    def step(tail):
        h = h_ref[...]
        gate = _dot(h, wg_ref[...])
        up = _dot(h, wu_ref[...])
        act = (gate / (1.0 + jnp.exp(-gate))) * up * 0.5
        wd = wd_ref[...]
        if tail:
            col = lax.broadcasted_iota(jnp.int32, act.shape, 1)
            act = jnp.where(col < FFN_TAIL, act, 0.0)
            row = lax.broadcasted_iota(jnp.int32, wd.shape, 0)
            wd = jnp.where(row < FFN_TAIL, wd, 0.0)
        o_ref[...] += _dot(act.astype(bf16), wd.astype(bf16))

    @pl.when(f < nf - 1)
    def _():
        step(False)

    @pl.when(f == nf - 1)
    def _():
        step(True)
        if final_norm:
            o_ref[...] = _rms(o_ref[...], gfin_ref[...])

            @pl.when(i == last_tile)
            def _():
                os_ref[...] = o_ref[FFN_PROMPT_TAIL:, :]


def _ffn(xs, g, wg, wu, wd, gfin, first):
    nf = FFN_NF
    tile = pl.BlockSpec((FFN_TM, D_MODEL), lambda i, f: (i, 0))
    sample = pl.BlockSpec((N_SAMPLE, D_MODEL), lambda i, f: (0, 0))
    vec = pl.BlockSpec((1, D_MODEL), lambda i, f: (0, 0))
    weights = [pl.BlockSpec((D_MODEL, FFN_TF), lambda i, f: (0, f)),
               pl.BlockSpec((D_MODEL, FFN_TF), lambda i, f: (0, f)),
               pl.BlockSpec((FFN_TF, D_MODEL), lambda i, f: (f, 0))]
    if first:
        in_specs = [tile, sample, vec] + weights + [vec]
        out_specs = tile
        out_shape = jax.ShapeDtypeStruct((N_ROWS, D_MODEL), f32)
    else:
        in_specs = [tile, vec] + weights + [vec]
        out_specs = [tile, sample]
        out_shape = [jax.ShapeDtypeStruct((N_PROMPT, D_MODEL), f32),
                     jax.ShapeDtypeStruct((N_SAMPLE, D_MODEL), f32)]
    return pl.pallas_call(
        functools.partial(_ffn_kernel, first=first, nf=nf),
        grid=(N_ROWS // FFN_TM, nf),
        in_specs=in_specs,
        out_specs=out_specs,
        out_shape=out_shape,
        scratch_shapes=[pltpu.VMEM((FFN_TM, D_MODEL), bf16)],
        compiler_params=pltpu.CompilerParams(
            dimension_semantics=("arbitrary", "arbitrary"), vmem_limit_bytes=VMEM_LIMIT),
        name="ffn_half",
    )(*xs, g, wg, wu, wd, gfin)


N_PROMPT_TILES = N_PROMPT // PROJ_TM
TILES_PER_SEQ = SEQ // PROJ_TM
assert N_SAMPLE == PROJ_TM


def _aproj_kernel(x_ref, g_ref, wq_ref, wk_ref, wv_ref, wf_ref, bf_ref,
                  qb_ref, kb_ref, kt_ref, vt_ref, vtb_ref, lf_ref, qs_ref, kts_ref, vts_ref):
    i = pl.program_id(0)
    h = _rms(x_ref[...], g_ref[...]).astype(bf16)
    q = _dot_nt(h, wq_ref[...]) * (HEAD_DIM ** -0.5)
    qb_ref[...] = (q * LOG2E).astype(bf16)
    kt = _dot_nt(wk_ref[...], h)
    kb_ref[...] = kt.T.astype(bf16)
    vt = _dot_nt(wv_ref[...], h)
    z = _dot_nt(h, wf_ref[...]) + bf_ref[...]
    lf_ref[...] = -(jnp.maximum(-z, 0.0) + jnp.log1p(jnp.exp(-jnp.abs(z))))

    @pl.when(i < N_PROMPT_TILES)
    def _():
        kt_ref[...] = kt
        vt_ref[...] = vt
        vtb_ref[...] = vt.astype(bf16)

    @pl.when(i == N_PROMPT_TILES)
    def _():
        qs_ref[...] = q
        kts_ref[...] = kt
        vts_ref[...] = vt


def _attn_proj(x, g, wt, wf, bfp):
    row = lambda i: (i, 0)
    fixed = lambda i: (0, 0)

    def seq_tile(i):
        t = jnp.minimum(i, N_PROMPT_TILES - 1)
        return (t // TILES_PER_SEQ, 0, t % TILES_PER_SEQ)

    wide = lambda dt: jax.ShapeDtypeStruct((N_ROWS, ATTN_WIDTH), dt)
    tposed = lambda dt: jax.ShapeDtypeStruct((BATCH, ATTN_WIDTH, SEQ), dt)
    small_t = jax.ShapeDtypeStruct((ATTN_WIDTH, N_SAMPLE), f32)
    w_block = lambda n: pl.BlockSpec((ATTN_WIDTH, D_MODEL), lambda i: (n, 0))
    return pl.pallas_call(
        _aproj_kernel,
        grid=(N_ROWS // PROJ_TM,),
        in_specs=[
            pl.BlockSpec((PROJ_TM, D_MODEL), row),
            pl.BlockSpec((1, D_MODEL), fixed),
            w_block(0), w_block(1), w_block(2),
            pl.BlockSpec((LANES, D_MODEL), fixed),
            pl.BlockSpec((1, LANES), fixed),
        ],
        out_specs=[pl.BlockSpec((PROJ_TM, ATTN_WIDTH), row)] * 2
        + [pl.BlockSpec((None, ATTN_WIDTH, PROJ_TM), seq_tile)] * 3
        + [pl.BlockSpec((PROJ_TM, LANES), row),
           pl.BlockSpec((N_SAMPLE, ATTN_WIDTH), fixed),
           pl.BlockSpec((ATTN_WIDTH, N_SAMPLE), fixed),
           pl.BlockSpec((ATTN_WIDTH, N_SAMPLE), fixed)],
        out_shape=[wide(bf16), wide(bf16), tposed(f32), tposed(f32), tposed(bf16),
                   jax.ShapeDtypeStruct((N_ROWS, LANES), f32),
                   jax.ShapeDtypeStruct((N_SAMPLE, ATTN_WIDTH), f32), small_t, small_t],
        compiler_params=pltpu.CompilerParams(
            dimension_semantics=("arbitrary",), vmem_limit_bytes=VMEM_LIMIT),
        name="attn_proj",
    )(x, g, wt, wt, wt, wf, bfp)


def _gelu_tanh(x):
    return 0.5 * x * (1.0 + jnp.tanh(0.7978845608028654 * (x + 0.044715 * (x * x * x))))


def _gmlp_kernel(x_ref, g_ref, wu_ref, wv_ref, ggv_ref, ws_ref, mask_ref, bs_ref,
                 o_ref, gvn_ref, *, n_tiles):
    i = pl.program_id(0)
    h = _rms(x_ref[...], g_ref[...]).astype(bf16)
    mask = mask_ref[...]
    bias = bs_ref[...]
    pair_w = 2 * GMLP_HEAD_DIM

    def project(gp):
        rows = slice(gp * pair_w, (gp + 1) * pair_w)
        return _dot_nt(h, wu_ref[rows, :]), _dot_nt(h, wv_ref[rows, :])

    def finish(gp, u_lin, gv_lin):
        u = _gelu_tanh(u_lin)
        gv = _gelu_tanh(gv_lin)
        for e in range(2):
            g = 2 * gp + e
            loc = slice(e * GMLP_HEAD_DIM, (e + 1) * GMLP_HEAD_DIM)
            cols = slice(g * GMLP_HEAD_DIM, (g + 1) * GMLP_HEAD_DIM)
            gvn = _rms(gv[:, loc], ggv_ref[g:g + 1, :])
            w = (ws_ref[g] * mask).astype(bf16)
            for c in range(PROJ_TM // CHUNK):
                rows = slice(c * CHUNK, (c + 1) * CHUNK)
                mixed = _dot(w, gvn[rows].astype(bf16)) + bias[:, g:g + 1]
                o_ref[rows, cols] = (u[rows, loc] * mixed).astype(bf16)

            @pl.when(i == n_tiles - 1)
            def _():
                gvn_ref[:, cols] = gvn

    n_pairs = N_GMLP_HEADS // 2
    cur = project(0)
    for gp in range(n_pairs):
        nxt = project(gp + 1) if gp + 1 < n_pairs else None
        finish(gp, *cur)
        cur = nxt


def _gmlp(x, g, wu, wv, ggv, ws2, mask2, bs2):
    n_tiles = N_ROWS // PROJ_TM
    n_prompt_tiles = N_PROMPT // PROJ_TM
    row = lambda i: (i, 0)
    fixed = lambda i: (0, 0)
    kind3 = lambda i: (i // n_prompt_tiles, 0, 0)
    kind4 = lambda i: (i // n_prompt_tiles, 0, 0, 0)
    return pl.pallas_call(
        functools.partial(_gmlp_kernel, n_tiles=n_tiles),
        grid=(n_tiles,),
        in_specs=[
            pl.BlockSpec((PROJ_TM, D_MODEL), row),
            pl.BlockSpec((1, D_MODEL), fixed),
            pl.BlockSpec((GMLP_WIDTH, D_MODEL), fixed),
            pl.BlockSpec((GMLP_WIDTH, D_MODEL), fixed),
            pl.BlockSpec((N_GMLP_HEADS, GMLP_HEAD_DIM), fixed),
            pl.BlockSpec((None, N_GMLP_HEADS, CHUNK, CHUNK), kind4),
            pl.BlockSpec((None, CHUNK, CHUNK), kind3),
            pl.BlockSpec((None, CHUNK, N_GMLP_HEADS), kind3),
        ],
        out_specs=[pl.BlockSpec((PROJ_TM, GMLP_WIDTH), row),
                   pl.BlockSpec((PROJ_TM, GMLP_WIDTH), fixed)],
        out_shape=[jax.ShapeDtypeStruct((N_ROWS, GMLP_WIDTH), bf16),
                   jax.ShapeDtypeStruct((N_SAMPLE, GMLP_WIDTH), f32)],
        compiler_params=pltpu.CompilerParams(
            dimension_semantics=("arbitrary",), vmem_limit_bytes=VMEM_LIMIT),
        name="gmlp_group",
    )(x, g, wu, wv, ggv, ws2, mask2, bs2)


def _cumsum_kernel(lf_ref, c_ref):
    r = lax.broadcasted_iota(jnp.int32, (CHUNK, CHUNK), 0)
    s = lax.broadcasted_iota(jnp.int32, (CHUNK, CHUNK), 1)
    tri = (s <= r).astype(bf16)

    def body(j, carry):
        rows = pl.ds(pl.multiple_of(j * CHUNK, CHUNK), CHUNK)
        hi, mid, lo = _split3(lf_ref[rows, :])
        c = _dot(tri, hi) + _dot(tri, mid) + _dot(tri, lo) + carry
        c_ref[rows, :] = c
        return c[CHUNK - 1:CHUNK, :]

    lax.fori_loop(0, SEQ // CHUNK, body, jnp.zeros((1, LANES), f32))


def _prompt_cumsum(lf_pad):
    return pl.pallas_call(
        _cumsum_kernel,
        grid=(BATCH,),
        in_specs=[pl.BlockSpec((SEQ, LANES), lambda b: (b, 0))],
        out_specs=pl.BlockSpec((SEQ, LANES), lambda b: (b, 0)),
        out_shape=jax.ShapeDtypeStruct((N_PROMPT, LANES), f32),
        compiler_params=pltpu.CompilerParams(
            dimension_semantics=("parallel",), vmem_limit_bytes=VMEM_LIMIT),
        name="logf_cumsum",
    )(lf_pad)


N_SPLIT = 3


def _bias_lanes(e):
    return HEAD_DIM * (1 - e)


def _fox_prompt_kernel(q_ref, k_ref, vt_ref, c_ref, o_ref,
                       kp_ref, g_ref, qp_ref, sa_ref, sb_ref, mxa_ref, mxb_ref, m_ref, acc_ref):
    hp = pl.program_id(1)
    qi = pl.program_id(2)
    t = ATT_T
    lane = lax.broadcasted_iota(jnp.int32, (t, LANES), 1)

    def in_span(e, first, n):
        base = _bias_lanes(e) + first
        return (lane >= base) & (lane < base + n)

    @pl.when(qi == 0)
    def _():
        r = lax.broadcasted_iota(jnp.int32, (LANES, LANES), 0)
        l2 = lax.broadcasted_iota(jnp.int32, (LANES, LANES), 1)
        sels = []
        for j in range(N_SPLIT):
            sel = jnp.zeros((LANES, LANES), f32)
            for e in range(2):
                src = r == 2 * hp + e
                sel = sel - (src & (l2 == _bias_lanes(e) + j)).astype(f32)
                sel = sel + (src & (l2 == _bias_lanes(e) + N_SPLIT + j)).astype(f32)
            sels.append(sel.astype(bf16))

        def prep(i, _):
            rows = pl.ds(pl.multiple_of(i * t, t), t)
            parts = _split3(c_ref[rows, :] * LOG2E)
            g = sum(_dot(p, s) for p, s in zip(parts, sels))
            g_ref[rows, :] = g
            k2 = k_ref[rows, :].astype(f32)
            for e in range(2):
                own = (lane < HEAD_DIM) if e == 0 else (lane >= HEAD_DIM)
                extra = jnp.where(in_span(e, 0, N_SPLIT), g,
                                  jnp.where(in_span(e, N_SPLIT, N_SPLIT), 1.0, 0.0))
                kp_ref[e, rows, :] = jnp.where(own, k2, extra).astype(bf16)
            return 0

        lax.fori_loop(0, SEQ // t, prep, 0)

    q0 = pl.multiple_of(qi * t, t)
    q2 = q_ref[...].astype(f32)
    gq = g_ref[pl.ds(q0, t), :]
    for e in range(2):
        own = (lane < HEAD_DIM) if e == 0 else (lane >= HEAD_DIM)
        extra = jnp.where(in_span(e, 0, N_SPLIT), 1.0,
                          jnp.where(in_span(e, N_SPLIT, N_SPLIT), gq, 0.0))
        qp_ref[e] = jnp.where(own, q2, extra).astype(bf16)

    m_ref[...] = jnp.full(m_ref.shape, NEG_INF, f32)
    acc_ref[...] = jnp.zeros(acc_ref.shape, f32)
    ones = jnp.ones((ATT_SUM_ROWS, t), bf16)

    def scores(blk, s_ref, mx_ref):
        k0 = pl.multiple_of(blk * t, t)
        for e in range(2):
            st = _dot_nt(kp_ref[e, pl.ds(k0, t), :], qp_ref[e])
            s_ref[e] = st
            mx_ref[e] = jnp.max(st, axis=0, keepdims=True)

    def consume(blk, s_ref, mx_ref, on_diagonal):
        k0 = pl.multiple_of(blk * t, t)
        for e in range(2):
            st = s_ref[e]
            if on_diagonal:
                key = lax.broadcasted_iota(jnp.int32, (t, t), 0)
                qry = lax.broadcasted_iota(jnp.int32, (t, t), 1)
                st = jnp.where(key <= qry, st, NEG_INF)
                mx = jnp.max(st, axis=0, keepdims=True)
            else:
                mx = mx_ref[e]
            m_old = m_ref[e]
            m_new = jnp.maximum(m_old, mx)
            alpha = jnp.exp2(m_old - m_new)
            pt = jnp.exp2(st - m_new)
            m_ref[e] = m_new
            vt = jnp.concatenate([vt_ref[e * HEAD_DIM:(e + 1) * HEAD_DIM, pl.ds(k0, t)], ones], axis=0)
            acc_ref[e] = acc_ref[e] * alpha + _dot(vt, pt.astype(bf16))

    scores(0, sa_ref, mxa_ref)

    def pair(p, _):
        j = 2 * p
        scores(j + 1, sb_ref, mxb_ref)
        consume(j, sa_ref, mxa_ref, False)
        scores(j + 2, sa_ref, mxa_ref)
        consume(j + 1, sb_ref, mxb_ref, False)
        return 0

    lax.fori_loop(0, qi // 2, pair, 0)

    @pl.when(qi % 2 == 0)
    def _():
        consume(qi, sa_ref, mxa_ref, True)

    @pl.when(qi % 2 == 1)
    def _():
        scores(qi, sb_ref, mxb_ref)
        consume(qi - 1, sa_ref, mxa_ref, False)
        consume(qi, sb_ref, mxb_ref, True)

    for e in range(2):
        acc = acc_ref[e]
        o_ref[e * HEAD_DIM:(e + 1) * HEAD_DIM, :] = (
            acc[0:HEAD_DIM] / acc[HEAD_DIM:HEAD_DIM + 1]).astype(bf16)


def _fox_prompt(qb, kb, vt, c):
    nq = SEQ // ATT_T
    return pl.pallas_call(
        _fox_prompt_kernel,
        grid=(BATCH, N_HEADS // 2, nq),
        in_specs=[
            pl.BlockSpec((ATT_T, LANES), lambda b, hp, qi: (b * nq + qi, hp)),
            pl.BlockSpec((SEQ, LANES), lambda b, hp, qi: (b, hp)),
            pl.BlockSpec((None, LANES, SEQ), lambda b, hp, qi: (b, hp, 0)),
            pl.BlockSpec((SEQ, LANES), lambda b, hp, qi: (b, 0)),
        ],
        out_specs=pl.BlockSpec((LANES, ATT_T), lambda b, hp, qi: (hp, b * nq + qi)),
        out_shape=jax.ShapeDtypeStruct((ATTN_WIDTH, N_PROMPT), bf16),
        scratch_shapes=[pltpu.VMEM((2, SEQ, LANES), bf16), pltpu.VMEM((SEQ, LANES), f32),
                        pltpu.VMEM((2, ATT_T, LANES), bf16),
                        pltpu.VMEM((2, ATT_T, ATT_T), f32), pltpu.VMEM((2, ATT_T, ATT_T), f32),
                        pltpu.VMEM((2, 1, ATT_T), f32), pltpu.VMEM((2, 1, ATT_T), f32),
                        pltpu.VMEM((2, 1, ATT_T), f32),
                        pltpu.VMEM((2, HEAD_DIM + ATT_SUM_ROWS, ATT_T), f32)],
        compiler_params=pltpu.CompilerParams(
            dimension_semantics=("parallel", "parallel", "arbitrary"),
            vmem_limit_bytes=VMEM_LIMIT),
        name="fox_prompt",
    )(qb, kb, vt, c)


def _fox_sample_kernel(pt_ref, q_ref, kn_ref, vn_ref, lfn_ref, *refs):
    g = PAGES_PER_STEP
    lf_refs = refs[:g]
    k_refs = refs[g:2 * g]
    v_refs = refs[2 * g:3 * g]
    o_ref, m_ref, l_ref, acc_ref, cn_ref, carry_ref = refs[3 * g:]
    j = pl.program_id(1)
    tok = lax.broadcasted_iota(jnp.int32, (DEC_SEQ, PAGE_SIZE), 0)
    key = lax.broadcasted_iota(jnp.int32, (DEC_SEQ, PAGE_SIZE), 1)
    j_i = lax.broadcasted_iota(jnp.int32, (PAGE_SIZE, PAGE_SIZE), 0)
    s_i = lax.broadcasted_iota(jnp.int32, (PAGE_SIZE, PAGE_SIZE), 1)

    @pl.when(j == 0)
    def _():
        carry_ref[...] = jnp.zeros(carry_ref.shape, f32)
        upto = (j_i <= s_i).astype(bf16)
        c_new = _dot3(lfn_ref[...], upto)
        own = []
        for h in range(N_HEADS):
            c_row = c_new[h:h + 1, :]
            c_col = jnp.sum(jnp.where(key == tok, c_row, 0.0), axis=1, keepdims=True)
            c_b = jnp.broadcast_to(c_col, (DEC_SEQ, PAGE_SIZE))
            cn_ref[h] = c_b
            s = _dot(q_ref[h], kn_ref[h]) + (c_b - c_row)
            own.append(jnp.where(key <= tok, s, NEG_INF))
        for h in range(N_HEADS):
            s = own[h]
            m = jnp.max(s, axis=1, keepdims=True)
            p = jnp.exp(s - m)
            m_ref[h] = jnp.broadcast_to(m, (DEC_SEQ, PAGE_SIZE))
            l_ref[h] = p
            acc_ref[h] = _dot_nt(p, vn_ref[h])

    after = (j_i > s_i).astype(bf16)
    x = jnp.concatenate([lf_refs[pg][...] for pg in range(g)], axis=0)
    within = _dot3(x, after)
    total = jnp.sum(x, axis=1, keepdims=True)
    carry = carry_ref[...]
    sfx = []
    for pg in range(g):
        rows = slice(pg * N_HEADS, (pg + 1) * N_HEADS)
        sfx.append(within[rows] + carry)
        carry = carry + total[rows]
    carry_ref[...] = carry

    scores, row_max = [], []
    for h in range(N_HEADS):
        q = q_ref[h]
        bias = cn_ref[h]
        s = [_dot(q, k_refs[pg][h]) + (bias + sfx[pg][h:h + 1, :]) for pg in range(g)]
        mx = s[0]
        for pg in range(1, g):
            mx = jnp.maximum(mx, s[pg])
        scores.append(s)
        row_max.append(jnp.max(mx, axis=1, keepdims=True))
    for h in range(N_HEADS):
        s = scores[h]
        m_old = m_ref[h]
        m_new = jnp.maximum(m_old, row_max[h])
        alpha = jnp.exp(m_old - m_new)
        p = [jnp.exp(s[pg] - m_new) for pg in range(g)]
        l_ref[h] = alpha * l_ref[h] + sum(p)
        m_ref[h] = m_new
        pv = sum(_dot_nt(p[pg], v_refs[pg][h]) for pg in range(g))
        acc_ref[h] = acc_ref[h] * alpha[:, :HEAD_DIM] + pv

    @pl.when(j == pl.num_programs(1) - 1)
    def _():
        for h in range(N_HEADS):
            o_ref[h] = acc_ref[h] / jnp.sum(l_ref[h], axis=1, keepdims=True)


def _fox_sample(page_table, q4, kn_t, vn_t, lfn_t, cache_lft, cache_kt, cache_vt):
    g = PAGES_PER_STEP
    nj = N_PAGES // g

    def page_of(b, j, pt, i):
        return pt[b, N_PAGES - 1 - (j * g + i)]

    per_b4 = lambda b, j, pt: (b, 0, 0, 0)
    page_spec = lambda i: pl.BlockSpec((None, N_HEADS, HEAD_DIM, PAGE_SIZE),
                                       lambda b, j, pt: (page_of(b, j, pt, i), 0, 0, 0))
    lf_spec = lambda i: pl.BlockSpec((None, N_HEADS, PAGE_SIZE),
                                     lambda b, j, pt: (page_of(b, j, pt, i), 0, 0))
    state = pltpu.VMEM((N_HEADS, DEC_SEQ, PAGE_SIZE), f32)
    grid_spec = pltpu.PrefetchScalarGridSpec(
        num_scalar_prefetch=1,
        grid=(DEC_BATCH, nj),
        in_specs=[
            pl.BlockSpec((None, N_HEADS, DEC_SEQ, HEAD_DIM), per_b4),
            pl.BlockSpec((None, N_HEADS, HEAD_DIM, PAGE_SIZE), per_b4),
            pl.BlockSpec((None, N_HEADS, HEAD_DIM, PAGE_SIZE), per_b4),
            pl.BlockSpec((None, N_HEADS, PAGE_SIZE), lambda b, j, pt: (b, 0, 0)),
        ] + [lf_spec(i) for i in range(g)] + [page_spec(i) for i in range(g)]
        + [page_spec(i) for i in range(g)],
        out_specs=pl.BlockSpec((None, N_HEADS, DEC_SEQ, HEAD_DIM), per_b4),
        scratch_shapes=[state, state, pltpu.VMEM((N_HEADS, DEC_SEQ, HEAD_DIM), f32), state,
                        pltpu.VMEM((N_HEADS, PAGE_SIZE), f32)],
    )
    return pl.pallas_call(
        _fox_sample_kernel,
        grid_spec=grid_spec,
        out_shape=jax.ShapeDtypeStruct((DEC_BATCH, N_HEADS, DEC_SEQ, HEAD_DIM), f32),
        compiler_params=pltpu.CompilerParams(
            dimension_semantics=("parallel", "arbitrary"), vmem_limit_bytes=VMEM_LIMIT),
        name="fox_sample",
    )(page_table, q4, kn_t, vn_t, lfn_t, *([cache_lft] * g), *([cache_kt] * g), *([cache_vt] * g))


def _oproj_kernel(x_ref, at_ref, as_ref, m_ref, wo_ref, o_ref):
    i = pl.program_id(0)
    wa = wo_ref[0:ATTN_WIDTH, :]
    base = x_ref[...] + _dot(m_ref[...], wo_ref[ATTN_WIDTH:, :])

    @pl.when(i < N_PROMPT_TILES)
    def _():
        o_ref[...] = base + lax.dot_general(at_ref[...], wa, (((0,), (0,)), ((), ())),
                                            preferred_element_type=f32)

    @pl.when(i == N_PROMPT_TILES)
    def _():
        o_ref[...] = base + _dot(as_ref[...], wa)


def _out_proj(x, att_t, att_s, gm, wo):
    row = lambda i: (i, 0)
    fixed = lambda i: (0, 0)
    return pl.pallas_call(
        _oproj_kernel,
        grid=(N_ROWS // PROJ_TM,),
        in_specs=[
            pl.BlockSpec((PROJ_TM, D_MODEL), row),
            pl.BlockSpec((ATTN_WIDTH, PROJ_TM), lambda i: (0, jnp.minimum(i, N_PROMPT_TILES - 1))),
            pl.BlockSpec((N_SAMPLE, ATTN_WIDTH), fixed),
            pl.BlockSpec((PROJ_TM, GMLP_WIDTH), row),
            pl.BlockSpec((ATTN_WIDTH + GMLP_WIDTH, D_MODEL), fixed),
        ],
        out_specs=pl.BlockSpec((PROJ_TM, D_MODEL), row),
        out_shape=jax.ShapeDtypeStruct((N_ROWS, D_MODEL), f32),
        compiler_params=pltpu.CompilerParams(
            dimension_semantics=("arbitrary",), vmem_limit_bytes=VMEM_LIMIT),
        name="out_proj",
    )(x, att_t, att_s, gm, wo)


def kernel(x_prompt, x_sample, cache_k, cache_v, cache_logf, page_table, g_ffn1, w1_gate, w1_up, w1_down, g_mix, w_in, b_f, g_gv, w_s, b_s, w_out, g_ffn2, w2_gate, w2_up, w2_down, g_final):
    l = 0
    x_in = (x_prompt.reshape(N_PROMPT, D_MODEL), x_sample.reshape(N_SAMPLE, D_MODEL))
    g_fin = g_final.reshape(1, D_MODEL)

    w1g, w1u, w1d = w1_gate[l].astype(bf16), w1_up[l].astype(bf16), w1_down[l]
    w2g, w2u, w2d = w2_gate[l].astype(bf16), w2_up[l].astype(bf16), w2_down[l]
    wt = w_in[l].T.astype(bf16)
    a = ATTN_WIDTH
    wf = jnp.pad(wt[3 * a:3 * a + N_HEADS], ((0, LANES - N_HEADS), (0, 0)))
    wu = wt[3 * a + N_HEADS:3 * a + N_HEADS + GMLP_WIDTH]
    wgv = wt[3 * a + N_HEADS + GMLP_WIDTH:]
    bfp = jnp.pad(b_f[l].reshape(1, N_HEADS), ((0, 0), (0, LANES - N_HEADS)))
    wo = w_out[l].astype(bf16)

    reps = CHUNK // DEC_SEQ
    ws2 = jnp.stack([w_s[l], jnp.tile(w_s[l][:, :DEC_SEQ, :DEC_SEQ], (1, reps, reps))])
    ti = jnp.arange(CHUNK)
    tril = (ti[None, :] <= ti[:, None])
    same = (ti[None, :] // DEC_SEQ) == (ti[:, None] // DEC_SEQ)
    mask2 = jnp.stack([tril, tril & same]).astype(f32)
    bs2 = jnp.stack([b_s[l].T, jnp.tile(b_s[l][:, :DEC_SEQ], (1, reps)).T])

    x1 = _ffn(x_in, g_ffn1[l].reshape(1, D_MODEL), w1g, w1u, w1d, g_fin, True)

    gm = g_mix[l].reshape(1, D_MODEL)
    qb, kb, kt_p, vt_p, vtb, lf_pad, q_s, kt_s, vt_s = _attn_proj(x1, gm, wt, wf, bfp)
    gmlp_out, gvn_s = _gmlp(x1, gm, wu, wgv, g_gv[l], ws2, mask2, bs2)

    c = _prompt_cumsum(lf_pad)
    att_t = _fox_prompt(qb, kb, vtb, c)

    def sample_keys_last(xt):
        return xt.reshape(N_HEADS, HEAD_DIM, DEC_BATCH, DEC_SEQ).transpose(2, 0, 1, 3)

    def as_page(x4):
        return jnp.pad(x4, ((0, 0), (0, 0), (0, 0), (0, PAGE_SIZE - DEC_SEQ)))

    k_s4, v_s4 = sample_keys_last(kt_s), sample_keys_last(vt_s)
    q4 = q_s.reshape(DEC_BATCH, DEC_SEQ, N_HEADS, HEAD_DIM).transpose(0, 2, 1, 3)
    lf = lf_pad[:, :N_HEADS]
    lfn_t = jnp.pad(lf[N_PROMPT:].reshape(DEC_BATCH, DEC_SEQ, N_HEADS).transpose(0, 2, 1),
                    ((0, 0), (0, 0), (0, PAGE_SIZE - DEC_SEQ)))
    cache_kt = cache_k[l].transpose(0, 2, 3, 1)
    cache_vt = cache_v[l].transpose(0, 2, 3, 1)
    cache_lft = cache_logf[l].transpose(0, 2, 1)
    att_s = _fox_sample(page_table, q4, as_page(k_s4), as_page(v_s4), lfn_t, cache_lft, cache_kt, cache_vt)
    att_s = att_s.transpose(0, 2, 1, 3).reshape(N_SAMPLE, ATTN_WIDTH).astype(bf16)

    x2 = _out_proj(x1, att_t, att_s, gmlp_out, wo)
    y_p, y_s = _ffn((x2,), g_ffn2[l].reshape(1, D_MODEL), w2g, w2u, w2d, g_fin, False)

    def prompt_rows(xt):
        return xt.reshape(BATCH, N_HEADS, HEAD_DIM, SEQ).transpose(0, 3, 1, 2)[None]

    def sample_rows(x4):
        return x4.transpose(0, 3, 1, 2)[None]

    return (y_p.reshape(BATCH, SEQ, D_MODEL),
            y_s.reshape(DEC_BATCH, DEC_SEQ, D_MODEL),
            prompt_rows(kt_p), prompt_rows(vt_p),
            lf[:N_PROMPT].reshape(1, BATCH, SEQ, N_HEADS),
            sample_rows(k_s4), sample_rows(v_s4),
            lf[N_PROMPT:].reshape(1, DEC_BATCH, DEC_SEQ, N_HEADS),
            gvn_s.reshape(1, DEC_BATCH, DEC_SEQ, N_GMLP_HEADS, GMLP_HEAD_DIM))
```
